```python
import jax, jax.numpy as jnp
from jax import lax
import numpy as np

D_MODEL = 1024
BATCH = 16
SEQ = 2048
DEPTH = 2
DEC_BATCH = 32
DEC_SEQ = 8
PAST_LEN = 16384
PAGE_SIZE = 128

HEAD_DIM = 64
DN_HEADS = 8
DN_WIDTH = DN_HEADS * HEAD_DIM
MOBA_HEADS = 8
MOBA_WIDTH = MOBA_HEADS * HEAD_DIM
SB_HEADS = 16
SB_WIDTH = SB_HEADS * HEAD_DIM
CONV_W = 4
DELTA_CHUNK = 64
MOBA_BLOCK = 256
MOBA_TOPK = 3
Q_BLOCK = 128
PEER_KEYS = 128
PEER_EXPERTS = PEER_KEYS * PEER_KEYS
PEER_HEADS = 8
PEER_TOPK = 16
PEER_QDIM = 128
PEER_CHUNK = 128
RMS_EPS = 1e-6
NEG = -1e30
ATTN_SCALE = HEAD_DIM ** -0.5
N_EVEN = (DEPTH + 1) // 2
N_ODD = DEPTH // 2
DN_QKV = 3 * DN_WIDTH
DN_IN = DN_QKV + DN_WIDTH + 2 * DN_HEADS
IN_EVEN = DN_IN + 3 * MOBA_WIDTH
IN_ODD = 3 * SB_WIDTH

kernel_name = "hybrid_deltanet_moba_stickbreaking_peer_step"


def rms_norm(x, w):
    xf = x.astype(jnp.float32)
    return xf * lax.rsqrt(jnp.mean(xf * xf, axis=-1, keepdims=True) + RMS_EPS) * w.astype(jnp.float32)


def l2_norm(x):
    return x * lax.rsqrt(jnp.sum(x * x, axis=-1, keepdims=True) + RMS_EPS)


def gather_pages(cache, layer, page_table):
    pages = cache[layer, page_table]
    b, n, p, h, d = pages.shape
    return pages.reshape(b, n * p, h, d)


def gated_delta_rule(q, k, v, g, beta, s0):
    B, L, H, d = q.shape
    c = min(DELTA_CHUNK, L)
    pad = (-L) % c
    if pad:
        pw4 = ((0, 0), (0, pad), (0, 0), (0, 0))
        pw3 = ((0, 0), (0, pad), (0, 0))
        q, k, v = jnp.pad(q, pw4), jnp.pad(k, pw4), jnp.pad(v, pw4)
        g, beta = jnp.pad(g, pw3), jnp.pad(beta, pw3)
    n = (L + pad) // c

    def to_chunks(t):
        return jnp.transpose(t.reshape(B, n, c, H, d), (1, 0, 3, 2, 4))

    qc = to_chunks(q) * ATTN_SCALE
    kc = to_chunks(k)
    vc = to_chunks(v)
    gc = jnp.cumsum(jnp.transpose(g.reshape(B, n, c, H), (1, 0, 3, 2)), axis=-1)
    bc = jnp.transpose(beta.reshape(B, n, c, H), (1, 0, 3, 2))
    incl = jnp.tril(jnp.ones((c, c), bool))
    strict = jnp.tril(jnp.ones((c, c), bool), -1)
    decay = jnp.exp(jnp.where(incl, gc[..., :, None] - gc[..., None, :], NEG))
    kb = kc * bc[..., None]
    lmat = jnp.where(strict, jnp.einsum('nbhid,nbhjd->nbhij', kb, kc) * decay, 0.0)
    eye = jnp.eye(c, dtype=jnp.float32)
    tmat = lax.linalg.triangular_solve(eye + lmat, jnp.broadcast_to(eye, lmat.shape), left_side=True, lower=True)
    u = jnp.einsum('nbhij,nbhjd->nbhid', tmat, vc * bc[..., None])
    w = jnp.einsum('nbhij,nbhjd->nbhid', tmat, kb * jnp.exp(gc)[..., None])
    intra = jnp.where(incl, jnp.einsum('nbhid,nbhjd->nbhij', qc, kc) * decay, 0.0)

    def step(s, xs):
        q_i, k_i, u_i, w_i, g_i, a_i = xs
        v_new = u_i - jnp.einsum('bhcd,bhde->bhce', w_i, s)
        o = (jnp.einsum('bhcd,bhde->bhce', q_i * jnp.exp(g_i)[..., None], s)
             + jnp.einsum('bhij,bhje->bhie', a_i, v_new))
        g_last = g_i[..., -1]
        s = (s * jnp.exp(g_last)[..., None, None]
             + jnp.einsum('bhcd,bhce->bhde', k_i * jnp.exp(g_last[..., None] - g_i)[..., None], v_new))
        return s, o

    s_fin, o = lax.scan(step, s0, (qc, kc, u, w, gc, intra))
    o = jnp.transpose(o, (1, 0, 3, 2, 4)).reshape(B, n * c, H, d)[:, :L]
    return o, s_fin


def deltanet_branch(proj, conv_buf, s0, conv_w, a_log, dt_bias, norm_w):
    B, L, _ = proj.shape
    raw = proj[..., :DN_QKV]
    z = proj[..., DN_QKV:DN_QKV + DN_WIDTH]
    a = proj[..., DN_QKV + DN_WIDTH:DN_QKV + DN_WIDTH + DN_HEADS]
    b = proj[..., DN_QKV + DN_WIDTH + DN_HEADS:DN_IN]
    xp = jnp.concatenate([conv_buf.astype(jnp.float32), raw], axis=1)
    conv = sum(xp[:, i:i + L] * conv_w[i].astype(jnp.float32) for i in range(CONV_W))
    qkv = jax.nn.silu(conv)
    q, k, v = jnp.split(qkv, 3, axis=-1)
    q = l2_norm(q.reshape(B, L, DN_HEADS, HEAD_DIM))
    k = l2_norm(k.reshape(B, L, DN_HEADS, HEAD_DIM))
    v = v.reshape(B, L, DN_HEADS, HEAD_DIM)
    g = -jnp.exp(a_log.astype(jnp.float32)) * jax.nn.softplus(a + dt_bias.astype(jnp.float32))
    beta = jax.nn.sigmoid(b)
    o, s_fin = gated_delta_rule(q, k, v, g, beta, s0.astype(jnp.float32))
    o = rms_norm(o, norm_w) * jax.nn.silu(z.reshape(B, L, DN_HEADS, HEAD_DIM))
    return o.reshape(B, L, DN_WIDTH), xp[:, L:], s_fin


def moba_blocks(k, v):
    B, L, H, d = k.shape
    pad = (-L) % MOBA_BLOCK
    pw = ((0, 0), (0, pad), (0, 0), (0, 0))
    kb = jnp.pad(k.astype(jnp.float32), pw).reshape(B, -1, MOBA_BLOCK, H, d)
    vb = jnp.pad(v.astype(jnp.float32), pw).reshape(B, -1, MOBA_BLOCK, H, d)
    means = jnp.mean(kb, axis=2)
    return kb, vb, means


def moba_attend(q, q_pos, kb, vb, means):
    nb = kb.shape[0]
    n_sel = min(MOBA_TOPK, nb)
    Q, H, _ = q.shape
    own = q_pos // MOBA_BLOCK
    qs = q * ATTN_SCALE
    blk = jnp.einsum('qhd,nhd->qhn', qs, means)
    cand = jnp.arange(nb)[None, None, :] < own[:, None, None]
    top_val, top_idx = lax.top_k(jnp.where(cand, blk, NEG), n_sel)
    sel = jnp.concatenate([top_idx, jnp.broadcast_to(own[:, None, None], (Q, H, 1))], axis=-1)
    slot_ok = jnp.concatenate([top_val > 0.5 * NEG, jnp.ones((Q, H, 1), bool)], axis=-1)
    s_idx = jnp.arange(MOBA_BLOCK)
    h_idx = jnp.arange(H)[None, :, None, None]
    gidx = sel[..., None]
    k_sel = kb[gidx, s_idx, h_idx]
    v_sel = vb[gidx, s_idx, h_idx]
    k_pos = gidx * MOBA_BLOCK + s_idx
    mask = slot_ok[..., None] & (k_pos <= q_pos[:, None, None, None])
    logits = jnp.where(mask, jnp.einsum('qhd,qhjsd->qhjs', qs, k_sel), NEG)
    p = jax.nn.softmax(logits.reshape(Q, H, -1), axis=-1).reshape(logits.shape)
    return jnp.einsum('qhjs,qhjsd->qhd', p, v_sel)


def moba_prompt(q, k, v):
    B, L, H, d = q.shape
    kb, vb, means = moba_blocks(k, v)
    pos = jnp.arange(L, dtype=jnp.int32).reshape(-1, Q_BLOCK)
    qblk = q.reshape(B, -1, Q_BLOCK, H, d)

    def per_seq(xs):
        q_s, kb_s, vb_s, m_s = xs
        return lax.map(lambda qp: moba_attend(qp[0], qp[1], kb_s, vb_s, m_s), (q_s, pos))

    return lax.map(per_seq, (qblk, kb, vb, means)).reshape(B, L, H, d)


def moba_sample(q, k_new, v_new, k_past, v_past):
    past_len = k_past.shape[1]
    kb, vb, means = moba_blocks(jnp.concatenate([k_past.astype(jnp.float32), k_new], axis=1),
                                jnp.concatenate([v_past.astype(jnp.float32), v_new], axis=1))
    q_pos = past_len + jnp.arange(q.shape[1], dtype=jnp.int32)
    return jax.vmap(moba_attend, in_axes=(0, None, 0, 0, 0))(q, q_pos, kb, vb, means)


def sb_segment(q, k, v, q_pos, k_pos, tail):
    z = jnp.einsum('bqhd,bkhd->bhqk', q, k.astype(jnp.float32)) * ATTN_SCALE
    causal = k_pos[None, :] < q_pos[:, None]
    log_1m = jnp.where(causal, jax.nn.log_sigmoid(-z), 0.0)
    later = lax.cumsum(log_1m, axis=3, reverse=True) - log_1m + tail[..., None]
    a = jnp.where(causal, jnp.exp(jax.nn.log_sigmoid(z) + later), 0.0)
    return jnp.einsum('bhqk,bkhd->bqhd', a, v.astype(jnp.float32)), tail + jnp.sum(log_1m, axis=3)


def sb_prompt(q, k, v):
    B, L, H, d = q.shape
    outs = []
    for i in range(L // Q_BLOCK):
        lo, hi = i * Q_BLOCK, (i + 1) * Q_BLOCK
        o, _ = sb_segment(q[:, lo:hi], k[:, :hi], v[:, :hi], jnp.arange(lo, hi), jnp.arange(hi),
                          jnp.zeros((B, H, Q_BLOCK), jnp.float32))
        outs.append(o)
    return jnp.concatenate(outs, axis=1)


def sb_sample(q, k_new, v_new, k_past, v_past):
    B, L, H, d = q.shape
    past_len = k_past.shape[1]
    q_pos = past_len + jnp.arange(L)
    o_new, tail = sb_segment(q, k_new, v_new, q_pos, q_pos, jnp.zeros((B, H, L), jnp.float32))
    o_past, _ = sb_segment(q, k_past, v_past, q_pos, jnp.arange(past_len), tail)
    return o_new + o_past


def peer_ffn(x, w_query, keys1, keys2, u_table, v_table):
    T, D = x.shape
    q = (x @ w_query.astype(jnp.float32)).reshape(T, PEER_HEADS, PEER_QDIM)
    half = PEER_QDIM // 2
    s1, i1 = lax.top_k(jnp.einsum('thd,hnd->thn', q[..., :half], keys1.astype(jnp.float32)), PEER_TOPK)
    s2, i2 = lax.top_k(jnp.einsum('thd,hnd->thn', q[..., half:], keys2.astype(jnp.float32)), PEER_TOPK)
    cand_s = (s1[..., :, None] + s2[..., None, :]).reshape(T, PEER_HEADS, PEER_TOPK * PEER_TOPK)
    cand_i = (i1[..., :, None] * PEER_KEYS + i2[..., None, :]).reshape(T, PEER_HEADS, PEER_TOPK * PEER_TOPK)
    top_s, pos = lax.top_k(cand_s, PEER_TOPK)
    experts = jnp.take_along_axis(cand_i, pos, axis=-1).reshape(T, -1)
    gates = jax.nn.softmax(top_s, axis=-1).reshape(T, -1)
    pad = (-T) % PEER_CHUNK
    xp = jnp.pad(x, ((0, pad), (0, 0))).reshape(-1, PEER_CHUNK, D)
    ep = jnp.pad(experts, ((0, pad), (0, 0))).reshape(-1, PEER_CHUNK, experts.shape[1])
    gp = jnp.pad(gates, ((0, pad), (0, 0))).reshape(-1, PEER_CHUNK, gates.shape[1])

    def chunk_fn(args):
        xc, ec, gc = args
        h = jax.nn.gelu(jnp.einsum('td,tkd->tk', xc, u_table[ec].astype(jnp.float32)), approximate=False)
        return jnp.einsum('tk,tkd->td', gc * h, v_table[ec].astype(jnp.float32))

    return lax.map(chunk_fn, (xp, ep, gp)).reshape(-1, D)[:T]


def even_mixer(hn, conv_buf, s0, past_kv, w_in, conv_w, a_log, dt_bias, norm_dn, q_norm, k_norm, w_out):
    B, L, _ = hn.shape
    proj = hn @ w_in.astype(jnp.float32)
    o_dn, new_buf, s_fin = deltanet_branch(proj[..., :DN_IN], conv_buf, s0, conv_w, a_log, dt_bias, norm_dn)
    mq, mk, mv = jnp.split(proj[..., DN_IN:], 3, axis=-1)
    q = rms_norm(mq.reshape(B, L, MOBA_HEADS, HEAD_DIM), q_norm)
    k = rms_norm(mk.reshape(B, L, MOBA_HEADS, HEAD_DIM), k_norm)
    v = mv.reshape(B, L, MOBA_HEADS, HEAD_DIM)
    if past_kv is None:
        o_moba = moba_prompt(q, k, v)
    else:
        o_moba = moba_sample(q, k, v, past_kv[0], past_kv[1])
    out = jnp.concatenate([o_dn, o_moba.reshape(B, L, MOBA_WIDTH)], axis=-1) @ w_out.astype(jnp.float32)
    return out, k, v, new_buf, s_fin


def odd_mixer(hn, past_kv, w_in, q_norm, k_norm, w_out):
    B, L, _ = hn.shape
    sq, sk, sv = jnp.split(hn @ w_in.astype(jnp.float32), 3, axis=-1)
    q = rms_norm(sq.reshape(B, L, SB_HEADS, HEAD_DIM), q_norm)
    k = rms_norm(sk.reshape(B, L, SB_HEADS, HEAD_DIM), k_norm)
    v = sv.reshape(B, L, SB_HEADS, HEAD_DIM)
    if past_kv is None:
        o = sb_prompt(q, k, v)
    else:
        o = sb_sample(q, k, v, past_kv[0], past_kv[1])
    return o.reshape(B, L, SB_WIDTH) @ w_out.astype(jnp.float32), k, v


def setup_inputs(seed: int = 0) -> dict:
    key = jax.random.key(seed)
    ks = jax.random.split(key, 32)
    f32 = jnp.float32
    n_pages = PAST_LEN // PAGE_SIZE
    n_phys = (DEC_BATCH * n_pages * 5) // 4

    def nrm(k, shape, scale):
        return jax.random.normal(k, shape, f32) * scale

    page_table = jax.random.permutation(ks[0], n_phys)[:DEC_BATCH * n_pages].reshape(DEC_BATCH, n_pages).astype(jnp.int32)
    return {
        "x_prompt": nrm(ks[1], (BATCH, SEQ, D_MODEL), 1.0),
        "x_sample": nrm(ks[2], (DEC_BATCH, DEC_SEQ, D_MODEL), 1.0),
        "cache_k_moba": nrm(ks[3], (N_EVEN, n_phys, PAGE_SIZE, MOBA_HEADS, HEAD_DIM), 1.0),
        "cache_v_moba": nrm(ks[4], (N_EVEN, n_phys, PAGE_SIZE, MOBA_HEADS, HEAD_DIM), 1.0),
        "state_conv_dn": nrm(ks[5], (N_EVEN, DEC_BATCH, CONV_W - 1, DN_QKV), 1.0),
        "state_delta_dn": nrm(ks[6], (N_EVEN, DEC_BATCH, DN_HEADS, HEAD_DIM, HEAD_DIM), 0.1),
        "cache_k_sb": nrm(ks[7], (N_ODD, n_phys, PAGE_SIZE, SB_HEADS, HEAD_DIM), 1.0),
        "cache_v_sb": nrm(ks[8], (N_ODD, n_phys, PAGE_SIZE, SB_HEADS, HEAD_DIM), 1.0),
        "page_table": page_table,
        "w_in_even": nrm(ks[9], (N_EVEN, D_MODEL, IN_EVEN), D_MODEL ** -0.5),
        "conv_w_dn": nrm(ks[10], (N_EVEN, CONV_W, DN_QKV), CONV_W ** -0.5),
        "a_log_dn": jnp.log(jax.random.uniform(ks[11], (N_EVEN, DN_HEADS), f32, 1.0, 16.0)),
        "dt_bias_dn": nrm(ks[12], (N_EVEN, DN_HEADS), 0.1),
        "norm_dn": 1.0 + nrm(ks[13], (N_EVEN, HEAD_DIM), 0.02),
        "q_norm_moba": 1.0 + nrm(ks[14], (N_EVEN, HEAD_DIM), 0.02),
        "k_norm_moba": 1.0 + nrm(ks[15], (N_EVEN, HEAD_DIM), 0.02),
        "w_out_even": nrm(ks[16], (N_EVEN, DN_WIDTH + MOBA_WIDTH, D_MODEL), (DN_WIDTH + MOBA_WIDTH) ** -0.5),
        "w_in_odd": nrm(ks[17], (N_ODD, D_MODEL, IN_ODD), D_MODEL ** -0.5),
        "q_norm_sb": 1.0 + nrm(ks[18], (N_ODD, HEAD_DIM), 0.02),
        "k_norm_sb": 1.0 + nrm(ks[19], (N_ODD, HEAD_DIM), 0.02),
        "w_out_odd": nrm(ks[20], (N_ODD, SB_WIDTH, D_MODEL), SB_WIDTH ** -0.5),
        "norm_mix": 1.0 + nrm(ks[21], (DEPTH, D_MODEL), 0.02),
        "norm_ffn": 1.0 + nrm(ks[22], (DEPTH, D_MODEL), 0.02),
        "peer_w_query": nrm(ks[23], (DEPTH, D_MODEL, PEER_HEADS * PEER_QDIM), D_MODEL ** -0.5),
        "peer_keys1": nrm(ks[24], (DEPTH, PEER_HEADS, PEER_KEYS, PEER_QDIM // 2), (PEER_QDIM // 2) ** -0.5),
        "peer_keys2": nrm(ks[25], (DEPTH, PEER_HEADS, PEER_KEYS, PEER_QDIM // 2), (PEER_QDIM // 2) ** -0.5),
        "peer_u": nrm(ks[26], (DEPTH, PEER_EXPERTS, D_MODEL), D_MODEL ** -0.5),
        "peer_v": nrm(ks[27], (DEPTH, PEER_EXPERTS, D_MODEL), 0.3),
    }


def reference(x_prompt, x_sample, cache_k_moba, cache_v_moba, state_conv_dn, state_delta_dn, cache_k_sb, cache_v_sb,
              page_table, w_in_even, conv_w_dn, a_log_dn, dt_bias_dn, norm_dn, q_norm_moba, k_norm_moba, w_out_even,
              w_in_odd, q_norm_sb, k_norm_sb, w_out_odd, norm_mix, norm_ffn, peer_w_query, peer_keys1, peer_keys2,
              peer_u, peer_v):

    def run(x, sample):
        B, L, _ = x.shape
        h = x.astype(jnp.float32)
        k_m, v_m, c_dn, s_dn, k_s, v_s = [], [], [], [], [], []
        for layer in range(DEPTH):
            li = layer // 2
            hn = rms_norm(h, norm_mix[layer])
            if layer % 2 == 0:
                if sample:
                    conv_buf, s0 = state_conv_dn[li], state_delta_dn[li]
                    past = (gather_pages(cache_k_moba, li, page_table), gather_pages(cache_v_moba, li, page_table))
                else:
                    conv_buf = jnp.zeros((B, CONV_W - 1, DN_QKV), jnp.float32)
                    s0 = jnp.zeros((B, DN_HEADS, HEAD_DIM, HEAD_DIM), jnp.float32)
                    past = None
                mix, k, v, buf, s = even_mixer(hn, conv_buf, s0, past, w_in_even[li], conv_w_dn[li], a_log_dn[li],
                                               dt_bias_dn[li], norm_dn[li], q_norm_moba[li], k_norm_moba[li],
                                               w_out_even[li])
                k_m.append(k.astype(cache_k_moba.dtype))
                v_m.append(v.astype(cache_v_moba.dtype))
                c_dn.append(buf.astype(state_conv_dn.dtype))
                s_dn.append(s.astype(state_delta_dn.dtype))
            else:
                past = ((gather_pages(cache_k_sb, li, page_table), gather_pages(cache_v_sb, li, page_table))
                        if sample else None)
                mix, k, v = odd_mixer(hn, past, w_in_odd[li], q_norm_sb[li], k_norm_sb[li], w_out_odd[li])
                k_s.append(k.astype(cache_k_sb.dtype))
                v_s.append(v.astype(cache_v_sb.dtype))
            h = h + mix
            hf = rms_norm(h, norm_ffn[layer]).reshape(B * L, D_MODEL)
            h = h + peer_ffn(hf, peer_w_query[layer], peer_keys1[layer], peer_keys2[layer], peer_u[layer],
                             peer_v[layer]).reshape(B, L, D_MODEL)
        return (h.astype(x.dtype), jnp.stack(k_m), jnp.stack(v_m), jnp.stack(c_dn), jnp.stack(s_dn),
                jnp.stack(k_s), jnp.stack(v_s))

    y_p, kmp, vmp, cdp, sdp, ksp, vsp = run(x_prompt, False)
    y_s, kms, vms, cds, sds, kss, vss = run(x_sample, True)
    return (y_p, y_s, kmp, vmp, cdp, sdp, ksp, vsp, kms, vms, cds, sds, kss, vss)
```

```python
import functools

import jax
import jax.numpy as jnp
from jax import lax
from jax.experimental import pallas as pl
from jax.experimental.pallas import tpu as pltpu

D_MODEL = 1024
HEAD_DIM = 64
DN_HEADS = 8
DN_WIDTH = DN_HEADS * HEAD_DIM
MOBA_HEADS = 8
MOBA_WIDTH = MOBA_HEADS * HEAD_DIM
SB_HEADS = 16
SB_WIDTH = SB_HEADS * HEAD_DIM
CONV_W = 4
DELTA_CHUNK = 64
MOBA_BLOCK = 256
MOBA_TOPK = 3
Q_BLOCK = 128
PEER_KEYS = 128
PEER_HEADS = 8
PEER_TOPK = 16
PEER_QDIM = 128
PEER_CHUNK = 128
RMS_EPS = 1e-6
NEG = -1e30
ATTN_SCALE = HEAD_DIM ** -0.5
DN_QKV = 3 * DN_WIDTH
DN_IN = DN_QKV + DN_WIDTH + 2 * DN_HEADS
IN_EVEN = DN_IN + 3 * MOBA_WIDTH
IN_ODD = 3 * SB_WIDTH

LANES = 128
VMEM_LIMIT = 56 * 1024 * 1024


def _norm_proj_kernel(x_ref, g_ref, w_ref, o_ref):
    x = x_ref[...]
    inv = lax.rsqrt(jnp.mean(x * x, axis=-1, keepdims=True) + RMS_EPS)
    xn = (x * inv * g_ref[...]).astype(jnp.bfloat16)
    o_ref[...] = jnp.dot(xn, w_ref[...], preferred_element_type=jnp.float32)


def norm_proj(x, gain, w, tm):
    m, d = x.shape
    n = w.shape[1]
    n_pad = -(-n // LANES) * LANES
    wb = jnp.pad(w.astype(jnp.bfloat16), ((0, 0), (0, n_pad - n)))
    out = pl.pallas_call(
        _norm_proj_kernel,
        grid=(m // tm,),
        in_specs=[
            pl.BlockSpec((tm, d), lambda i: (i, 0)),
            pl.BlockSpec((1, d), lambda i: (0, 0)),
            pl.BlockSpec((d, n_pad), lambda i: (0, 0)),
        ],
        out_specs=pl.BlockSpec((tm, n_pad), lambda i: (i, 0)),
        out_shape=jax.ShapeDtypeStruct((m, n_pad), jnp.float32),
        compiler_params=pltpu.CompilerParams(
            dimension_semantics=("parallel",), vmem_limit_bytes=VMEM_LIMIT),
        name="norm_proj",
    )(x, gain.reshape(1, d).astype(jnp.float32), wb)
    return out[:, :n]


def rms_norm(x, w):
    xf = x.astype(jnp.float32)
    return xf * lax.rsqrt(jnp.mean(xf * xf, axis=-1, keepdims=True) + RMS_EPS) * w.astype(jnp.float32)


def l2_norm(x):
    return x * lax.rsqrt(jnp.sum(x * x, axis=-1, keepdims=True) + RMS_EPS)


def gather_pages(cache, layer, page_table):
    pages = cache[layer, page_table]
    b, n, p, h, d = pages.shape
    return pages.reshape(b, n * p, h, d)


def gated_delta_rule(q, k, v, g, beta, s0):
    B, L, H, d = q.shape
    c = min(DELTA_CHUNK, L)
    pad = (-L) % c
    if pad:
        pw4 = ((0, 0), (0, pad), (0, 0), (0, 0))
        pw3 = ((0, 0), (0, pad), (0, 0))
        q, k, v = jnp.pad(q, pw4), jnp.pad(k, pw4), jnp.pad(v, pw4)
        g, beta = jnp.pad(g, pw3), jnp.pad(beta, pw3)
    n = (L + pad) // c

    def to_chunks(t):
        return jnp.transpose(t.reshape(B, n, c, H, d), (1, 0, 3, 2, 4))

    qc = to_chunks(q) * ATTN_SCALE
    kc = to_chunks(k)
    vc = to_chunks(v)
    gc = jnp.cumsum(jnp.transpose(g.reshape(B, n, c, H), (1, 0, 3, 2)), axis=-1)
    bc = jnp.transpose(beta.reshape(B, n, c, H), (1, 0, 3, 2))
    incl = jnp.tril(jnp.ones((c, c), bool))
    strict = jnp.tril(jnp.ones((c, c), bool), -1)
    decay = jnp.exp(jnp.where(incl, gc[..., :, None] - gc[..., None, :], NEG))
    kb = kc * bc[..., None]
    lmat = jnp.where(strict, jnp.einsum('nbhid,nbhjd->nbhij', kb, kc) * decay, 0.0)
    eye = jnp.eye(c, dtype=jnp.float32)
    tmat = lax.linalg.triangular_solve(eye + lmat, jnp.broadcast_to(eye, lmat.shape), left_side=True, lower=True)
    u = jnp.einsum('nbhij,nbhjd->nbhid', tmat, vc * bc[..., None])
    w = jnp.einsum('nbhij,nbhjd->nbhid', tmat, kb * jnp.exp(gc)[..., None])
    intra = jnp.where(incl, jnp.einsum('nbhid,nbhjd->nbhij', qc, kc) * decay, 0.0)

    def step(s, xs):
        q_i, k_i, u_i, w_i, g_i, a_i = xs
        v_new = u_i - jnp.einsum('bhcd,bhde->bhce', w_i, s)
        o = (jnp.einsum('bhcd,bhde->bhce', q_i * jnp.exp(g_i)[..., None], s)
             + jnp.einsum('bhij,bhje->bhie', a_i, v_new))
        g_last = g_i[..., -1]
        s = (s * jnp.exp(g_last)[..., None, None]
             + jnp.einsum('bhcd,bhce->bhde', k_i * jnp.exp(g_last[..., None] - g_i)[..., None], v_new))
        return s, o

    s_fin, o = lax.scan(step, s0, (qc, kc, u, w, gc, intra))
    o = jnp.transpose(o, (1, 0, 3, 2, 4)).reshape(B, n * c, H, d)[:, :L]
    return o, s_fin


def deltanet_branch(proj, conv_buf, s0, conv_w, a_log, dt_bias, norm_w):
    B, L, _ = proj.shape
    raw = proj[..., :DN_QKV]
    z = proj[..., DN_QKV:DN_QKV + DN_WIDTH]
    a = proj[..., DN_QKV + DN_WIDTH:DN_QKV + DN_WIDTH + DN_HEADS]
    b = proj[..., DN_QKV + DN_WIDTH + DN_HEADS:DN_IN]
    xp = jnp.concatenate([conv_buf.astype(jnp.float32), raw], axis=1)
    conv = sum(xp[:, i:i + L] * conv_w[i].astype(jnp.float32) for i in range(CONV_W))
    qkv = jax.nn.silu(conv)
    q, k, v = jnp.split(qkv, 3, axis=-1)
    q = l2_norm(q.reshape(B, L, DN_HEADS, HEAD_DIM))
    k = l2_norm(k.reshape(B, L, DN_HEADS, HEAD_DIM))
    v = v.reshape(B, L, DN_HEADS, HEAD_DIM)
    g = -jnp.exp(a_log.astype(jnp.float32)) * jax.nn.softplus(a + dt_bias.astype(jnp.float32))
    beta = jax.nn.sigmoid(b)
    o, s_fin = gated_delta_rule(q, k, v, g, beta, s0.astype(jnp.float32))
    o = rms_norm(o, norm_w) * jax.nn.silu(z.reshape(B, L, DN_HEADS, HEAD_DIM))
    return o.reshape(B, L, DN_WIDTH), xp[:, L:], s_fin


def moba_blocks(k, v):
    B, L, H, d = k.shape
    pad = (-L) % MOBA_BLOCK
    pw = ((0, 0), (0, pad), (0, 0), (0, 0))
    kb = jnp.pad(k.astype(jnp.float32), pw).reshape(B, -1, MOBA_BLOCK, H, d)
    vb = jnp.pad(v.astype(jnp.float32), pw).reshape(B, -1, MOBA_BLOCK, H, d)
    means = jnp.mean(kb, axis=2)
    return kb, vb, means


def moba_attend(q, q_pos, kb, vb, means):
    nb = kb.shape[0]
    n_sel = min(MOBA_TOPK, nb)
    Q, H, _ = q.shape
    own = q_pos // MOBA_BLOCK
    qs = q * ATTN_SCALE
    blk = jnp.einsum('qhd,nhd->qhn', qs, means)
    cand = jnp.arange(nb)[None, None, :] < own[:, None, None]
    top_val, top_idx = lax.top_k(jnp.where(cand, blk, NEG), n_sel)
    sel = jnp.concatenate([top_idx, jnp.broadcast_to(own[:, None, None], (Q, H, 1))], axis=-1)
    slot_ok = jnp.concatenate([top_val > 0.5 * NEG, jnp.ones((Q, H, 1), bool)], axis=-1)
    s_idx = jnp.arange(MOBA_BLOCK)
    h_idx = jnp.arange(H)[None, :, None, None]
    gidx = sel[..., None]
    k_sel = kb[gidx, s_idx, h_idx]
    v_sel = vb[gidx, s_idx, h_idx]
    k_pos = gidx * MOBA_BLOCK + s_idx
    mask = slot_ok[..., None] & (k_pos <= q_pos[:, None, None, None])
    logits = jnp.where(mask, jnp.einsum('qhd,qhjsd->qhjs', qs, k_sel), NEG)
    p = jax.nn.softmax(logits.reshape(Q, H, -1), axis=-1).reshape(logits.shape)
    return jnp.einsum('qhjs,qhjsd->qhd', p, v_sel)


def moba_prompt(q, k, v):
    B, L, H, d = q.shape
    kb, vb, means = moba_blocks(k, v)
    pos = jnp.arange(L, dtype=jnp.int32).reshape(-1, Q_BLOCK)
    qblk = q.reshape(B, -1, Q_BLOCK, H, d)

    def per_seq(xs):
        q_s, kb_s, vb_s, m_s = xs
        return lax.map(lambda qp: moba_attend(qp[0], qp[1], kb_s, vb_s, m_s), (q_s, pos))

    return lax.map(per_seq, (qblk, kb, vb, means)).reshape(B, L, H, d)


def moba_sample(q, k_new, v_new, k_past, v_past):
    past_len = k_past.shape[1]
    kb, vb, means = moba_blocks(jnp.concatenate([k_past.astype(jnp.float32), k_new], axis=1),
                                jnp.concatenate([v_past.astype(jnp.float32), v_new], axis=1))
    q_pos = past_len + jnp.arange(q.shape[1], dtype=jnp.int32)
    return jax.vmap(moba_attend, in_axes=(0, None, 0, 0, 0))(q, q_pos, kb, vb, means)


def sb_segment(q, k, v, q_pos, k_pos, tail):
    z = jnp.einsum('bqhd,bkhd->bhqk', q, k.astype(jnp.float32)) * ATTN_SCALE
    causal = k_pos[None, :] < q_pos[:, None]
    log_1m = jnp.where(causal, jax.nn.log_sigmoid(-z), 0.0)
    later = lax.cumsum(log_1m, axis=3, reverse=True) - log_1m + tail[..., None]
    a = jnp.where(causal, jnp.exp(jax.nn.log_sigmoid(z) + later), 0.0)
    return jnp.einsum('bhqk,bkhd->bqhd', a, v.astype(jnp.float32)), tail + jnp.sum(log_1m, axis=3)


def sb_prompt(q, k, v):
    B, L, H, d = q.shape
    outs = []
    for i in range(L // Q_BLOCK):
        lo, hi = i * Q_BLOCK, (i + 1) * Q_BLOCK
        o, _ = sb_segment(q[:, lo:hi], k[:, :hi], v[:, :hi], jnp.arange(lo, hi), jnp.arange(hi),
                          jnp.zeros((B, H, Q_BLOCK), jnp.float32))
        outs.append(o)
    return jnp.concatenate(outs, axis=1)


def sb_sample(q, k_new, v_new, k_past, v_past):
    B, L, H, d = q.shape
    past_len = k_past.shape[1]
    q_pos = past_len + jnp.arange(L)
    o_new, tail = sb_segment(q, k_new, v_new, q_pos, q_pos, jnp.zeros((B, H, L), jnp.float32))
    o_past, _ = sb_segment(q, k_past, v_past, q_pos, jnp.arange(past_len), tail)
    return o_new + o_past


def peer_ffn(x, w_query, keys1, keys2, u_table, v_table):
    T, D = x.shape
    q = (x @ w_query.astype(jnp.float32)).reshape(T, PEER_HEADS, PEER_QDIM)
    half = PEER_QDIM // 2
    s1, i1 = lax.top_k(jnp.einsum('thd,hnd->thn', q[..., :half], keys1.astype(jnp.float32)), PEER_TOPK)
    s2, i2 = lax.top_k(jnp.einsum('thd,hnd->thn', q[..., half:], keys2.astype(jnp.float32)), PEER_TOPK)
    cand_s = (s1[..., :, None] + s2[..., None, :]).reshape(T, PEER_HEADS, PEER_TOPK * PEER_TOPK)
    cand_i = (i1[..., :, None] * PEER_KEYS + i2[..., None, :]).reshape(T, PEER_HEADS, PEER_TOPK * PEER_TOPK)
    top_s, pos = lax.top_k(cand_s, PEER_TOPK)
    experts = jnp.take_along_axis(cand_i, pos, axis=-1).reshape(T, -1)
    gates = jax.nn.softmax(top_s, axis=-1).reshape(T, -1)
    pad = (-T) % PEER_CHUNK
    xp = jnp.pad(x, ((0, pad), (0, 0))).reshape(-1, PEER_CHUNK, D)
    ep = jnp.pad(experts, ((0, pad), (0, 0))).reshape(-1, PEER_CHUNK, experts.shape[1])
    gp = jnp.pad(gates, ((0, pad), (0, 0))).reshape(-1, PEER_CHUNK, gates.shape[1])

    def chunk_fn(args):
        xc, ec, gc = args
        h = jax.nn.gelu(jnp.einsum('td,tkd->tk', xc, u_table[ec].astype(jnp.float32)), approximate=False)
        return jnp.einsum('tk,tkd->td', gc * h, v_table[ec].astype(jnp.float32))

    return lax.map(chunk_fn, (xp, ep, gp)).reshape(-1, D)[:T]


def even_mixer(h, norm_w, conv_buf, s0, past_kv, w_in, conv_w, a_log, dt_bias, norm_dn, q_norm, k_norm, w_out):
    B, L, _ = h.shape
    tm = 256 if B * L >= 256 else B * L
    proj = norm_proj(h.reshape(B * L, D_MODEL), norm_w, w_in, tm).reshape(B, L, IN_EVEN)
    o_dn, new_buf, s_fin = deltanet_branch(proj[..., :DN_IN], conv_buf, s0, conv_w, a_log, dt_bias, norm_dn)
    mq, mk, mv = jnp.split(proj[..., DN_IN:], 3, axis=-1)
    q = rms_norm(mq.reshape(B, L, MOBA_HEADS, HEAD_DIM), q_norm)
    k = rms_norm(mk.reshape(B, L, MOBA_HEADS, HEAD_DIM), k_norm)
    v = mv.reshape(B, L, MOBA_HEADS, HEAD_DIM)
    if past_kv is None:
        o_moba = moba_prompt(q, k, v)
    else:
        o_moba = moba_sample(q, k, v, past_kv[0], past_kv[1])
    out = jnp.concatenate([o_dn, o_moba.reshape(B, L, MOBA_WIDTH)], axis=-1) @ w_out.astype(jnp.float32)
    return out, k, v, new_buf, s_fin


def odd_mixer(h, norm_w, past_kv, w_in, q_norm, k_norm, w_out):
    B, L, _ = h.shape
    tm = 256 if B * L >= 256 else B * L
    proj = norm_proj(h.reshape(B * L, D_MODEL), norm_w, w_in, tm).reshape(B, L, IN_ODD)
    sq, sk, sv = jnp.split(proj, 3, axis=-1)
    q = rms_norm(sq.reshape(B, L, SB_HEADS, HEAD_DIM), q_norm)
    k = rms_norm(sk.reshape(B, L, SB_HEADS, HEAD_DIM), k_norm)
    v = sv.reshape(B, L, SB_HEADS, HEAD_DIM)
    if past_kv is None:
        o = sb_prompt(q, k, v)
    else:
        o = sb_sample(q, k, v, past_kv[0], past_kv[1])
    return o.reshape(B, L, SB_WIDTH) @ w_out.astype(jnp.float32), k, v


def kernel(x_prompt, x_sample, cache_k_moba, cache_v_moba, state_conv_dn, state_delta_dn, cache_k_sb, cache_v_sb,
           page_table, w_in_even, conv_w_dn, a_log_dn, dt_bias_dn, norm_dn, q_norm_moba, k_norm_moba, w_out_even,
           w_in_odd, q_norm_sb, k_norm_sb, w_out_odd, norm_mix, norm_ffn, peer_w_query, peer_keys1, peer_keys2,
           peer_u, peer_v):
    depth = norm_mix.shape[0]

    def run(x, sample):
        B, L, _ = x.shape
        h = x.astype(jnp.float32)
        k_m, v_m, c_dn, s_dn, k_s, v_s = [], [], [], [], [], []
        for layer in range(depth):
            li = layer // 2
            if layer % 2 == 0:
                if sample:
                    conv_buf, s0 = state_conv_dn[li], state_delta_dn[li]
                    past = (gather_pages(cache_k_moba, li, page_table), gather_pages(cache_v_moba, li, page_table))
                else:
                    conv_buf = jnp.zeros((B, CONV_W - 1, DN_QKV), jnp.float32)
                    s0 = jnp.zeros((B, DN_HEADS, HEAD_DIM, HEAD_DIM), jnp.float32)
                    past = None
                mix, k, v, buf, s = even_mixer(h, norm_mix[layer], conv_buf, s0, past, w_in_even[li], conv_w_dn[li],
                                               a_log_dn[li], dt_bias_dn[li], norm_dn[li], q_norm_moba[li],
                                               k_norm_moba[li], w_out_even[li])
                k_m.append(k)
                v_m.append(v)
                c_dn.append(buf)
                s_dn.append(s)
            else:
                past = ((gather_pages(cache_k_sb, li, page_table), gather_pages(cache_v_sb, li, page_table))
                        if sample else None)
                mix, k, v = odd_mixer(h, norm_mix[layer], past, w_in_odd[li], q_norm_sb[li], k_norm_sb[li],
                                      w_out_odd[li])
                k_s.append(k)
                v_s.append(v)
            h = h + mix
            hf = rms_norm(h, norm_ffn[layer]).reshape(B * L, D_MODEL)
            h = h + peer_ffn(hf, peer_w_query[layer], peer_keys1[layer], peer_keys2[layer], peer_u[layer],
                             peer_v[layer]).reshape(B, L, D_MODEL)
        return (h, jnp.stack(k_m), jnp.stack(v_m), jnp.stack(c_dn), jnp.stack(s_dn), jnp.stack(k_s), jnp.stack(v_s))

    y_p, kmp, vmp, cdp, sdp, ksp, vsp = run(x_prompt, False)
    y_s, kms, vms, cds, sds, kss, vss = run(x_sample, True)
    return (y_p, y_s, kmp, vmp, cdp, sdp, ksp, vsp, kms, vms, cds, sds, kss, vss)
```

```python
import functools

import jax
import jax.numpy as jnp
from jax import lax
from jax.experimental import pallas as pl
from jax.experimental.pallas import tpu as pltpu

D_MODEL = 1024
HEAD_DIM = 64
DN_HEADS = 8
DN_WIDTH = DN_HEADS * HEAD_DIM
MOBA_HEADS = 8
MOBA_WIDTH = MOBA_HEADS * HEAD_DIM
SB_HEADS = 16
SB_WIDTH = SB_HEADS * HEAD_DIM
CONV_W = 4
DELTA_CHUNK = 64
MOBA_BLOCK = 256
MOBA_TOPK = 3
Q_BLOCK = 128
PEER_KEYS = 128
PEER_HEADS = 8
PEER_TOPK = 16
PEER_QDIM = 128
PEER_CHUNK = 128
RMS_EPS = 1e-6
NEG = -1e30
ATTN_SCALE = HEAD_DIM ** -0.5
DN_QKV = 3 * DN_WIDTH
DN_IN = DN_QKV + DN_WIDTH + 2 * DN_HEADS
IN_EVEN = DN_IN + 3 * MOBA_WIDTH
IN_ODD = 3 * SB_WIDTH

LANES = 128
VMEM_LIMIT = 56 * 1024 * 1024


def _norm_proj_kernel(x_ref, g_ref, w_ref, o_ref):
    x = x_ref[...]
    inv = lax.rsqrt(jnp.mean(x * x, axis=-1, keepdims=True) + RMS_EPS)
    xn = (x * inv * g_ref[...]).astype(jnp.bfloat16)
    o_ref[...] = jnp.dot(xn, w_ref[...], preferred_element_type=jnp.float32)


def norm_proj(x, gain, w, tm):
    m, d = x.shape
    n = w.shape[1]
    n_pad = -(-n // LANES) * LANES
    wb = jnp.pad(w.astype(jnp.bfloat16), ((0, 0), (0, n_pad - n)))
    out = pl.pallas_call(
        _norm_proj_kernel,
        grid=(m // tm,),
        in_specs=[
            pl.BlockSpec((tm, d), lambda i: (i, 0)),
            pl.BlockSpec((1, d), lambda i: (0, 0)),
            pl.BlockSpec((d, n_pad), lambda i: (0, 0)),
        ],
        out_specs=pl.BlockSpec((tm, n_pad), lambda i: (i, 0)),
        out_shape=jax.ShapeDtypeStruct((m, n_pad), jnp.float32),
        compiler_params=pltpu.CompilerParams(
            dimension_semantics=("parallel",), vmem_limit_bytes=VMEM_LIMIT),
        name="norm_proj",
    )(x, gain.reshape(1, d).astype(jnp.float32), wb)
    return out[:, :n]


_NT = (((1,), (1,)), ((), ()))


def _moba_prompt_kernel(q_ref, k_ref, v_ref, o_ref, means_ref, kb_ref, vb_ref):
    i = pl.program_id(2)
    nb = means_ref.shape[0]
    blk_len = MOBA_BLOCK

    @pl.when(i == 0)
    def _():
        for n in range(nb):
            means_ref[n:n + 1, :] = jnp.mean(k_ref[0, 0, n * blk_len:(n + 1) * blk_len, :], axis=0, keepdims=True)
        kb_ref[...] = k_ref[0, 0].astype(jnp.bfloat16)
        vb_ref[...] = v_ref[0, 0].astype(jnp.bfloat16)

    qs = q_ref[0, 0] * ATTN_SCALE
    blk = lax.dot_general(qs, means_ref[...], _NT, precision=lax.Precision.HIGHEST,
                          preferred_element_type=jnp.float32)
    col = lax.broadcasted_iota(jnp.int32, (blk_len, nb), 1)
    cand = col < i
    blk = jnp.where(cand, blk, NEG)
    rank = jnp.zeros((blk_len, nb), jnp.int32)
    for m in range(nb):
        bm = blk[:, m:m + 1]
        rank += ((bm > blk) | ((bm == blk) & (m < col))).astype(jnp.int32)
    sel = (cand & (rank < MOBA_TOPK) & (blk > 0.5 * NEG)).astype(jnp.float32)

    qb = qs.astype(jnp.bfloat16)
    own = pl.multiple_of(i * blk_len, blk_len)
    s = lax.dot_general(qb, kb_ref[pl.ds(own, blk_len), :], _NT, preferred_element_type=jnp.float32)
    row = lax.broadcasted_iota(jnp.int32, (blk_len, blk_len), 0)
    colk = lax.broadcasted_iota(jnp.int32, (blk_len, blk_len), 1)
    s = jnp.where(colk <= row, s, NEG)
    m0 = jnp.max(s, axis=1, keepdims=True)
    p = jnp.exp(s - m0)
    l0 = jnp.sum(p, axis=1, keepdims=True)
    acc0 = jnp.dot(p.astype(jnp.bfloat16), vb_ref[pl.ds(own, blk_len), :], preferred_element_type=jnp.float32)

    def body(n, carry):
        m_i, l_i, acc = carry
        start = pl.multiple_of(n * blk_len, blk_len)
        s = lax.dot_general(qb, kb_ref[pl.ds(start, blk_len), :], _NT, preferred_element_type=jnp.float32)
        chosen = jnp.sum(jnp.where(col == n, sel, 0.0), axis=1, keepdims=True) > 0.5
        s = jnp.where(chosen, s, NEG)
        m_new = jnp.maximum(m_i, jnp.max(s, axis=1, keepdims=True))
        alpha = jnp.exp(m_i - m_new)
        p = jnp.exp(s - m_new)
        l_new = alpha * l_i + jnp.sum(p, axis=1, keepdims=True)
        acc = alpha * acc + jnp.dot(p.astype(jnp.bfloat16), vb_ref[pl.ds(start, blk_len), :],
                                    preferred_element_type=jnp.float32)
        return m_new, l_new, acc

    _, l_f, acc_f = lax.fori_loop(0, i, body, (m0, l0, acc0))
    o_ref[0, 0] = acc_f / l_f


def moba_prompt_attention(q, k, v):
    B, L, H, d = q.shape
    nb = L // MOBA_BLOCK
    qt, kt, vt = (jnp.transpose(t, (0, 2, 1, 3)) for t in (q, k, v))
    full = pl.BlockSpec((1, 1, L, d), lambda b, h, i: (b, h, 0, 0))
    tile = pl.BlockSpec((1, 1, MOBA_BLOCK, d), lambda b, h, i: (b, h, i, 0))
    o = pl.pallas_call(
        _moba_prompt_kernel,
        grid=(B, H, nb),
        in_specs=[tile, full, full],
        out_specs=tile,
        out_shape=jax.ShapeDtypeStruct((B, H, L, d), jnp.float32),
        scratch_shapes=[pltpu.VMEM((nb, d), jnp.float32), pltpu.VMEM((L, d), jnp.bfloat16),
                        pltpu.VMEM((L, d), jnp.bfloat16)],
        compiler_params=pltpu.CompilerParams(
            dimension_semantics=("parallel", "parallel", "arbitrary"), vmem_limit_bytes=VMEM_LIMIT),
        name="moba_prompt",
    )(qt, kt, vt)
    return jnp.transpose(o, (0, 2, 1, 3))


def rms_norm(x, w):
    xf = x.astype(jnp.float32)
    return xf * lax.rsqrt(jnp.mean(xf * xf, axis=-1, keepdims=True) + RMS_EPS) * w.astype(jnp.float32)


def l2_norm(x):
    return x * lax.rsqrt(jnp.sum(x * x, axis=-1, keepdims=True) + RMS_EPS)


def gather_pages(cache, layer, page_table):
    pages = cache[layer, page_table]
    b, n, p, h, d = pages.shape
    return pages.reshape(b, n * p, h, d)


def gated_delta_rule(q, k, v, g, beta, s0):
    B, L, H, d = q.shape
    c = min(DELTA_CHUNK, L)
    pad = (-L) % c
    if pad:
        pw4 = ((0, 0), (0, pad), (0, 0), (0, 0))
        pw3 = ((0, 0), (0, pad), (0, 0))
        q, k, v = jnp.pad(q, pw4), jnp.pad(k, pw4), jnp.pad(v, pw4)
        g, beta = jnp.pad(g, pw3), jnp.pad(beta, pw3)
    n = (L + pad) // c

    def to_chunks(t):
        return jnp.transpose(t.reshape(B, n, c, H, d), (1, 0, 3, 2, 4))

    qc = to_chunks(q) * ATTN_SCALE
    kc = to_chunks(k)
    vc = to_chunks(v)
    gc = jnp.cumsum(jnp.transpose(g.reshape(B, n, c, H), (1, 0, 3, 2)), axis=-1)
    bc = jnp.transpose(beta.reshape(B, n, c, H), (1, 0, 3, 2))
    incl = jnp.tril(jnp.ones((c, c), bool))
    strict = jnp.tril(jnp.ones((c, c), bool), -1)
    decay = jnp.exp(jnp.where(incl, gc[..., :, None] - gc[..., None, :], NEG))
    kb = kc * bc[..., None]
    lmat = jnp.where(strict, jnp.einsum('nbhid,nbhjd->nbhij', kb, kc) * decay, 0.0)
    eye = jnp.eye(c, dtype=jnp.float32)
    tmat = lax.linalg.triangular_solve(eye + lmat, jnp.broadcast_to(eye, lmat.shape), left_side=True, lower=True)
    u = jnp.einsum('nbhij,nbhjd->nbhid', tmat, vc * bc[..., None])
    w = jnp.einsum('nbhij,nbhjd->nbhid', tmat, kb * jnp.exp(gc)[..., None])
    intra = jnp.where(incl, jnp.einsum('nbhid,nbhjd->nbhij', qc, kc) * decay, 0.0)

    def step(s, xs):
        q_i, k_i, u_i, w_i, g_i, a_i = xs
        v_new = u_i - jnp.einsum('bhcd,bhde->bhce', w_i, s)
        o = (jnp.einsum('bhcd,bhde->bhce', q_i * jnp.exp(g_i)[..., None], s)
             + jnp.einsum('bhij,bhje->bhie', a_i, v_new))
        g_last = g_i[..., -1]
        s = (s * jnp.exp(g_last)[..., None, None]
             + jnp.einsum('bhcd,bhce->bhde', k_i * jnp.exp(g_last[..., None] - g_i)[..., None], v_new))
        return s, o

    s_fin, o = lax.scan(step, s0, (qc, kc, u, w, gc, intra))
    o = jnp.transpose(o, (1, 0, 3, 2, 4)).reshape(B, n * c, H, d)[:, :L]
    return o, s_fin


def deltanet_branch(proj, conv_buf, s0, conv_w, a_log, dt_bias, norm_w):
    B, L, _ = proj.shape
    raw = proj[..., :DN_QKV]
    z = proj[..., DN_QKV:DN_QKV + DN_WIDTH]
    a = proj[..., DN_QKV + DN_WIDTH:DN_QKV + DN_WIDTH + DN_HEADS]
    b = proj[..., DN_QKV + DN_WIDTH + DN_HEADS:DN_IN]
    xp = jnp.concatenate([conv_buf.astype(jnp.float32), raw], axis=1)
    conv = sum(xp[:, i:i + L] * conv_w[i].astype(jnp.float32) for i in range(CONV_W))
    qkv = jax.nn.silu(conv)
    q, k, v = jnp.split(qkv, 3, axis=-1)
    q = l2_norm(q.reshape(B, L, DN_HEADS, HEAD_DIM))
    k = l2_norm(k.reshape(B, L, DN_HEADS, HEAD_DIM))
    v = v.reshape(B, L, DN_HEADS, HEAD_DIM)
    g = -jnp.exp(a_log.astype(jnp.float32)) * jax.nn.softplus(a + dt_bias.astype(jnp.float32))
    beta = jax.nn.sigmoid(b)
    o, s_fin = gated_delta_rule(q, k, v, g, beta, s0.astype(jnp.float32))
    o = rms_norm(o, norm_w) * jax.nn.silu(z.reshape(B, L, DN_HEADS, HEAD_DIM))
    return o.reshape(B, L, DN_WIDTH), xp[:, L:], s_fin


def moba_blocks(k, v):
    B, L, H, d = k.shape
    pad = (-L) % MOBA_BLOCK
    pw = ((0, 0), (0, pad), (0, 0), (0, 0))
    kb = jnp.pad(k.astype(jnp.float32), pw).reshape(B, -1, MOBA_BLOCK, H, d)
    vb = jnp.pad(v.astype(jnp.float32), pw).reshape(B, -1, MOBA_BLOCK, H, d)
    means = jnp.mean(kb, axis=2)
    return kb, vb, means


def moba_attend(q, q_pos, kb, vb, means):
    nb = kb.shape[0]
    n_sel = min(MOBA_TOPK, nb)
    Q, H, _ = q.shape
    own = q_pos // MOBA_BLOCK
    qs = q * ATTN_SCALE
    blk = jnp.einsum('qhd,nhd->qhn', qs, means)
    cand = jnp.arange(nb)[None, None, :] < own[:, None, None]
    top_val, top_idx = lax.top_k(jnp.where(cand, blk, NEG), n_sel)
    sel = jnp.concatenate([top_idx, jnp.broadcast_to(own[:, None, None], (Q, H, 1))], axis=-1)
    slot_ok = jnp.concatenate([top_val > 0.5 * NEG, jnp.ones((Q, H, 1), bool)], axis=-1)
    s_idx = jnp.arange(MOBA_BLOCK)
    h_idx = jnp.arange(H)[None, :, None, None]
    gidx = sel[..., None]
    k_sel = kb[gidx, s_idx, h_idx]
    v_sel = vb[gidx, s_idx, h_idx]
    k_pos = gidx * MOBA_BLOCK + s_idx
    mask = slot_ok[..., None] & (k_pos <= q_pos[:, None, None, None])
    logits = jnp.where(mask, jnp.einsum('qhd,qhjsd->qhjs', qs, k_sel), NEG)
    p = jax.nn.softmax(logits.reshape(Q, H, -1), axis=-1).reshape(logits.shape)
    return jnp.einsum('qhjs,qhjsd->qhd', p, v_sel)


def moba_prompt(q, k, v):
    B, L, H, d = q.shape
    kb, vb, means = moba_blocks(k, v)
    pos = jnp.arange(L, dtype=jnp.int32).reshape(-1, Q_BLOCK)
    qblk = q.reshape(B, -1, Q_BLOCK, H, d)

    def per_seq(xs):
        q_s, kb_s, vb_s, m_s = xs
        return lax.map(lambda qp: moba_attend(qp[0], qp[1], kb_s, vb_s, m_s), (q_s, pos))

    return lax.map(per_seq, (qblk, kb, vb, means)).reshape(B, L, H, d)


def moba_sample(q, k_new, v_new, k_past, v_past):
    past_len = k_past.shape[1]
    kb, vb, means = moba_blocks(jnp.concatenate([k_past.astype(jnp.float32), k_new], axis=1),
                                jnp.concatenate([v_past.astype(jnp.float32), v_new], axis=1))
    q_pos = past_len + jnp.arange(q.shape[1], dtype=jnp.int32)
    return jax.vmap(moba_attend, in_axes=(0, None, 0, 0, 0))(q, q_pos, kb, vb, means)


def sb_segment(q, k, v, q_pos, k_pos, tail):
    z = jnp.einsum('bqhd,bkhd->bhqk', q, k.astype(jnp.float32)) * ATTN_SCALE
    causal = k_pos[None, :] < q_pos[:, None]
    log_1m = jnp.where(causal, jax.nn.log_sigmoid(-z), 0.0)
    later = lax.cumsum(log_1m, axis=3, reverse=True) - log_1m + tail[..., None]
    a = jnp.where(causal, jnp.exp(jax.nn.log_sigmoid(z) + later), 0.0)
    return jnp.einsum('bhqk,bkhd->bqhd', a, v.astype(jnp.float32)), tail + jnp.sum(log_1m, axis=3)


def sb_prompt(q, k, v):
    B, L, H, d = q.shape
    outs = []
    for i in range(L // Q_BLOCK):
        lo, hi = i * Q_BLOCK, (i + 1) * Q_BLOCK
        o, _ = sb_segment(q[:, lo:hi], k[:, :hi], v[:, :hi], jnp.arange(lo, hi), jnp.arange(hi),
                          jnp.zeros((B, H, Q_BLOCK), jnp.float32))
        outs.append(o)
    return jnp.concatenate(outs, axis=1)


def sb_sample(q, k_new, v_new, k_past, v_past):
    B, L, H, d = q.shape
    past_len = k_past.shape[1]
    q_pos = past_len + jnp.arange(L)
    o_new, tail = sb_segment(q, k_new, v_new, q_pos, q_pos, jnp.zeros((B, H, L), jnp.float32))
    o_past, _ = sb_segment(q, k_past, v_past, q_pos, jnp.arange(past_len), tail)
    return o_new + o_past


def peer_ffn(x, w_query, keys1, keys2, u_table, v_table):
    T, D = x.shape
    q = (x @ w_query.astype(jnp.float32)).reshape(T, PEER_HEADS, PEER_QDIM)
    half = PEER_QDIM // 2
    s1, i1 = lax.top_k(jnp.einsum('thd,hnd->thn', q[..., :half], keys1.astype(jnp.float32)), PEER_TOPK)
    s2, i2 = lax.top_k(jnp.einsum('thd,hnd->thn', q[..., half:], keys2.astype(jnp.float32)), PEER_TOPK)
    cand_s = (s1[..., :, None] + s2[..., None, :]).reshape(T, PEER_HEADS, PEER_TOPK * PEER_TOPK)
    cand_i = (i1[..., :, None] * PEER_KEYS + i2[..., None, :]).reshape(T, PEER_HEADS, PEER_TOPK * PEER_TOPK)
    top_s, pos = lax.top_k(cand_s, PEER_TOPK)
    experts = jnp.take_along_axis(cand_i, pos, axis=-1).reshape(T, -1)
    gates = jax.nn.softmax(top_s, axis=-1).reshape(T, -1)
    pad = (-T) % PEER_CHUNK
    xp = jnp.pad(x, ((0, pad), (0, 0))).reshape(-1, PEER_CHUNK, D)
    ep = jnp.pad(experts, ((0, pad), (0, 0))).reshape(-1, PEER_CHUNK, experts.shape[1])
    gp = jnp.pad(gates, ((0, pad), (0, 0))).reshape(-1, PEER_CHUNK, gates.shape[1])

    def chunk_fn(args):
        xc, ec, gc = args
        h = jax.nn.gelu(jnp.einsum('td,tkd->tk', xc, u_table[ec].astype(jnp.float32)), approximate=False)
        return jnp.einsum('tk,tkd->td', gc * h, v_table[ec].astype(jnp.float32))

    return lax.map(chunk_fn, (xp, ep, gp)).reshape(-1, D)[:T]


def even_mixer(h, norm_w, conv_buf, s0, past_kv, w_in, conv_w, a_log, dt_bias, norm_dn, q_norm, k_norm, w_out):
    B, L, _ = h.shape
    tm = 256 if B * L >= 256 else B * L
    proj = norm_proj(h.reshape(B * L, D_MODEL), norm_w, w_in, tm).reshape(B, L, IN_EVEN)
    o_dn, new_buf, s_fin = deltanet_branch(proj[..., :DN_IN], conv_buf, s0, conv_w, a_log, dt_bias, norm_dn)
    mq, mk, mv = jnp.split(proj[..., DN_IN:], 3, axis=-1)
    q = rms_norm(mq.reshape(B, L, MOBA_HEADS, HEAD_DIM), q_norm)
    k = rms_norm(mk.reshape(B, L, MOBA_HEADS, HEAD_DIM), k_norm)
    v = mv.reshape(B, L, MOBA_HEADS, HEAD_DIM)
    if past_kv is None:
        o_moba = moba_prompt_attention(q, k, v)
    else:
        o_moba = moba_sample(q, k, v, past_kv[0], past_kv[1])
    out = jnp.concatenate([o_dn, o_moba.reshape(B, L, MOBA_WIDTH)], axis=-1) @ w_out.astype(jnp.float32)
    return out, k, v, new_buf, s_fin


def odd_mixer(h, norm_w, past_kv, w_in, q_norm, k_norm, w_out):
    B, L, _ = h.shape
    tm = 256 if B * L >= 256 else B * L
    proj = norm_proj(h.reshape(B * L, D_MODEL), norm_w, w_in, tm).reshape(B, L, IN_ODD)
    sq, sk, sv = jnp.split(proj, 3, axis=-1)
    q = rms_norm(sq.reshape(B, L, SB_HEADS, HEAD_DIM), q_norm)
    k = rms_norm(sk.reshape(B, L, SB_HEADS, HEAD_DIM), k_norm)
    v = sv.reshape(B, L, SB_HEADS, HEAD_DIM)
    if past_kv is None:
        o = sb_prompt(q, k, v)
    else:
        o = sb_sample(q, k, v, past_kv[0], past_kv[1])
    return o.reshape(B, L, SB_WIDTH) @ w_out.astype(jnp.float32), k, v


def kernel(x_prompt, x_sample, cache_k_moba, cache_v_moba, state_conv_dn, state_delta_dn, cache_k_sb, cache_v_sb,
           page_table, w_in_even, conv_w_dn, a_log_dn, dt_bias_dn, norm_dn, q_norm_moba, k_norm_moba, w_out_even,
           w_in_odd, q_norm_sb, k_norm_sb, w_out_odd, norm_mix, norm_ffn, peer_w_query, peer_keys1, peer_keys2,
           peer_u, peer_v):
    depth = norm_mix.shape[0]

    def run(x, sample):
        B, L, _ = x.shape
        h = x.astype(jnp.float32)
        k_m, v_m, c_dn, s_dn, k_s, v_s = [], [], [], [], [], []
        for layer in range(depth):
            li = layer // 2
            if layer % 2 == 0:
                if sample:
                    conv_buf, s0 = state_conv_dn[li], state_delta_dn[li]
                    past = (gather_pages(cache_k_moba, li, page_table), gather_pages(cache_v_moba, li, page_table))
                else:
                    conv_buf = jnp.zeros((B, CONV_W - 1, DN_QKV), jnp.float32)
                    s0 = jnp.zeros((B, DN_HEADS, HEAD_DIM, HEAD_DIM), jnp.float32)
                    past = None
                mix, k, v, buf, s = even_mixer(h, norm_mix[layer], conv_buf, s0, past, w_in_even[li], conv_w_dn[li],
                                               a_log_dn[li], dt_bias_dn[li], norm_dn[li], q_norm_moba[li],
                                               k_norm_moba[li], w_out_even[li])
                k_m.append(k)
                v_m.append(v)
                c_dn.append(buf)
                s_dn.append(s)
            else:
                past = ((gather_pages(cache_k_sb, li, page_table), gather_pages(cache_v_sb, li, page_table))
                        if sample else None)
                mix, k, v = odd_mixer(h, norm_mix[layer], past, w_in_odd[li], q_norm_sb[li], k_norm_sb[li],
                                      w_out_odd[li])
                k_s.append(k)
                v_s.append(v)
            h = h + mix
            hf = rms_norm(h, norm_ffn[layer]).reshape(B * L, D_MODEL)
            h = h + peer_ffn(hf, peer_w_query[layer], peer_keys1[layer], peer_keys2[layer], peer_u[layer],
                             peer_v[layer]).reshape(B, L, D_MODEL)
        return (h, jnp.stack(k_m), jnp.stack(v_m), jnp.stack(c_dn), jnp.stack(s_dn), jnp.stack(k_s), jnp.stack(v_s))

    y_p, kmp, vmp, cdp, sdp, ksp, vsp = run(x_prompt, False)
    y_s, kms, vms, cds, sds, kss, vss = run(x_sample, True)
    return (y_p, y_s, kmp, vmp, cdp, sdp, ksp, vsp, kms, vms, cds, sds, kss, vss)
```

```python
import functools

import jax
import jax.numpy as jnp
from jax import lax
from jax.experimental import pallas as pl
from jax.experimental.pallas import tpu as pltpu

D_MODEL = 1024
HEAD_DIM = 64
DN_HEADS = 8
DN_WIDTH = DN_HEADS * HEAD_DIM
MOBA_HEADS = 8
MOBA_WIDTH = MOBA_HEADS * HEAD_DIM
SB_HEADS = 16
SB_WIDTH = SB_HEADS * HEAD_DIM
CONV_W = 4
DELTA_CHUNK = 64
MOBA_BLOCK = 256
MOBA_TOPK = 3
Q_BLOCK = 128
PEER_KEYS = 128
PEER_HEADS = 8
PEER_TOPK = 16
PEER_QDIM = 128
PEER_CHUNK = 128
RMS_EPS = 1e-6
NEG = -1e30
ATTN_SCALE = HEAD_DIM ** -0.5
DN_QKV = 3 * DN_WIDTH
DN_IN = DN_QKV + DN_WIDTH + 2 * DN_HEADS
IN_EVEN = DN_IN + 3 * MOBA_WIDTH
IN_ODD = 3 * SB_WIDTH

LANES = 128
VMEM_LIMIT = 56 * 1024 * 1024


def _norm_proj_kernel(x_ref, g_ref, w_ref, o_ref):
    x = x_ref[...]
    inv = lax.rsqrt(jnp.mean(x * x, axis=-1, keepdims=True) + RMS_EPS)
    xn = (x * inv * g_ref[...]).astype(jnp.bfloat16)
    o_ref[...] = jnp.dot(xn, w_ref[...], preferred_element_type=jnp.float32)


def norm_proj(x, gain, w, tm):
    m, d = x.shape
    n = w.shape[1]
    n_pad = -(-n // LANES) * LANES
    wb = jnp.pad(w.astype(jnp.bfloat16), ((0, 0), (0, n_pad - n)))
    out = pl.pallas_call(
        _norm_proj_kernel,
        grid=(m // tm,),
        in_specs=[
            pl.BlockSpec((tm, d), lambda i: (i, 0)),
            pl.BlockSpec((1, d), lambda i: (0, 0)),
            pl.BlockSpec((d, n_pad), lambda i: (0, 0)),
        ],
        out_specs=pl.BlockSpec((tm, n_pad), lambda i: (i, 0)),
        out_shape=jax.ShapeDtypeStruct((m, n_pad), jnp.float32),
        compiler_params=pltpu.CompilerParams(
            dimension_semantics=("parallel",), vmem_limit_bytes=VMEM_LIMIT),
        name="norm_proj",
    )(x, gain.reshape(1, d).astype(jnp.float32), wb)
    return out[:, :n]


_NT = (((1,), (1,)), ((), ()))


def _moba_prompt_kernel(q_ref, k_ref, v_ref, o_ref, means_ref, kb_ref, vb_ref):
    i = pl.program_id(2)
    nb = means_ref.shape[0]
    blk_len = MOBA_BLOCK

    @pl.when(i == 0)
    def _():
        for n in range(nb):
            means_ref[n:n + 1, :] = jnp.mean(k_ref[0, 0, n * blk_len:(n + 1) * blk_len, :], axis=0, keepdims=True)
        kb_ref[...] = k_ref[0, 0].astype(jnp.bfloat16)
        vb_ref[...] = v_ref[0, 0].astype(jnp.bfloat16)

    qs = q_ref[0, 0] * ATTN_SCALE
    blk = lax.dot_general(qs, means_ref[...], _NT, precision=lax.Precision.HIGHEST,
                          preferred_element_type=jnp.float32)
    col = lax.broadcasted_iota(jnp.int32, (blk_len, nb), 1)
    cand = col < i
    blk = jnp.where(cand, blk, NEG)
    rank = jnp.zeros((blk_len, nb), jnp.int32)
    for m in range(nb):
        bm = blk[:, m:m + 1]
        rank += ((bm > blk) | ((bm == blk) & (m < col))).astype(jnp.int32)
    sel = (cand & (rank < MOBA_TOPK) & (blk > 0.5 * NEG)).astype(jnp.float32)

    qb = qs.astype(jnp.bfloat16)
    own = pl.multiple_of(i * blk_len, blk_len)
    s = lax.dot_general(qb, kb_ref[pl.ds(own, blk_len), :], _NT, preferred_element_type=jnp.float32)
    row = lax.broadcasted_iota(jnp.int32, (blk_len, blk_len), 0)
    colk = lax.broadcasted_iota(jnp.int32, (blk_len, blk_len), 1)
    s = jnp.where(colk <= row, s, NEG)
    m0 = jnp.max(s, axis=1, keepdims=True)
    p = jnp.exp(s - m0)
    l0 = jnp.sum(p, axis=1, keepdims=True)
    acc0 = jnp.dot(p.astype(jnp.bfloat16), vb_ref[pl.ds(own, blk_len), :], preferred_element_type=jnp.float32)

    def body(n, carry):
        m_i, l_i, acc = carry
        start = pl.multiple_of(n * blk_len, blk_len)
        s = lax.dot_general(qb, kb_ref[pl.ds(start, blk_len), :], _NT, preferred_element_type=jnp.float32)
        chosen = jnp.sum(jnp.where(col == n, sel, 0.0), axis=1, keepdims=True) > 0.5
        s = jnp.where(chosen, s, NEG)
        m_new = jnp.maximum(m_i, jnp.max(s, axis=1, keepdims=True))
        alpha = jnp.exp(m_i - m_new)
        p = jnp.exp(s - m_new)
        l_new = alpha * l_i + jnp.sum(p, axis=1, keepdims=True)
        acc = alpha * acc + jnp.dot(p.astype(jnp.bfloat16), vb_ref[pl.ds(start, blk_len), :],
                                    preferred_element_type=jnp.float32)
        return m_new, l_new, acc

    _, l_f, acc_f = lax.fori_loop(0, i, body, (m0, l0, acc0))
    o_ref[0, 0] = acc_f / l_f


def moba_prompt_attention(q, k, v):
    B, L, H, d = q.shape
    nb = L // MOBA_BLOCK
    qt, kt, vt = (jnp.transpose(t, (0, 2, 1, 3)) for t in (q, k, v))
    full = pl.BlockSpec((1, 1, L, d), lambda b, h, i: (b, h, 0, 0))
    tile = pl.BlockSpec((1, 1, MOBA_BLOCK, d), lambda b, h, i: (b, h, i, 0))
    o = pl.pallas_call(
        _moba_prompt_kernel,
        grid=(B, H, nb),
        in_specs=[tile, full, full],
        out_specs=tile,
        out_shape=jax.ShapeDtypeStruct((B, H, L, d), jnp.float32),
        scratch_shapes=[pltpu.VMEM((nb, d), jnp.float32), pltpu.VMEM((L, d), jnp.bfloat16),
                        pltpu.VMEM((L, d), jnp.bfloat16)],
        compiler_params=pltpu.CompilerParams(
            dimension_semantics=("parallel", "parallel", "arbitrary"), vmem_limit_bytes=VMEM_LIMIT),
        name="moba_prompt",
    )(qt, kt, vt)
    return jnp.transpose(o, (0, 2, 1, 3))


SUBLANES = 8
HALF = SUBLANES // 2
PEER_TB = 128
PEER_SLOTS = PEER_HEADS * PEER_TOPK
PEER_ROWS = PEER_SLOTS * SUBLANES
PACK_ROWS = 2 * SUBLANES
PEER_UNROLL = 2
HI_MASK = 0xFFFF0000


def pack_expert_table(tbl):
    e, d = tbl.shape
    bits = lax.bitcast_convert_type(tbl.astype(jnp.bfloat16), jnp.uint16).astype(jnp.uint32)
    bits = bits.reshape(e // 2, 2, 2, HALF, LANES)
    words = bits[:, :, 0] | (bits[:, :, 1] << 16)
    return words.reshape(e // 2 * SUBLANES, LANES)


def _unpack_pair(w):
    lo = lax.bitcast_convert_type(w << 16, jnp.float32)
    hi = lax.bitcast_convert_type(w & jnp.uint32(HI_MASK), jnp.float32)
    return lo, hi


def _load_table_and_rows(tbl_hbm, tbl_vmem, exp_ref, rows_vmem, rows_smem, sem):
    @pl.when(pl.program_id(0) == 0)
    def _():
        cp = pltpu.make_async_copy(tbl_hbm, tbl_vmem, sem.at[0])
        cp.start()
        cp.wait()

    rows_vmem[...] = (exp_ref[...] >> 1) * SUBLANES
    cp = pltpu.make_async_copy(rows_vmem, rows_smem, sem.at[1])
    cp.start()
    cp.wait()


def _gather_slabs(tbl_vmem, rows_smem, t, glo_ref, ghi_ref):
    for m in range(PEER_SLOTS // 2):
        halves = []
        for k in (2 * m, 2 * m + 1):
            row = pl.multiple_of(rows_smem[t, k], SUBLANES)
            halves.append(_unpack_pair(tbl_vmem[pl.ds(row, SUBLANES), :]))
        rows = pl.ds(m * PACK_ROWS, PACK_ROWS)
        glo_ref[rows, :] = jnp.concatenate([halves[0][0], halves[1][0]], axis=0).astype(jnp.bfloat16)
        ghi_ref[rows, :] = jnp.concatenate([halves[0][1], halves[1][1]], axis=0).astype(jnp.bfloat16)


def _expanded_parity(exp_ref, spread_ref):
    par = (exp_ref[...] & 1).astype(jnp.float32)
    return jnp.dot(par, spread_ref[...], preferred_element_type=jnp.float32)


def _peer_u_kernel(exp_ref, x_ref, spread_ref, tbl_hbm, h_ref, tbl_vmem, rows_vmem, rows_smem, glo_ref, ghi_ref,
                   r_ref, x16_ref, sem):
    _load_table_and_rows(tbl_hbm, tbl_vmem, exp_ref, rows_vmem, rows_smem, sem)
    tb = x_ref.shape[0]
    for r in range(PACK_ROWS):
        blk = r % HALF + HALF * (r // SUBLANES)
        x16_ref[pl.ds(r, tb, stride=PACK_ROWS), :] = x_ref[:, blk * LANES:(blk + 1) * LANES]
    row = lax.broadcasted_iota(jnp.int32, (PACK_ROWS, PEER_ROWS), 0)
    lane_sub = lax.broadcasted_iota(jnp.int32, (PACK_ROWS, PEER_ROWS), 1) % SUBLANES
    mask_lo = (row == lane_sub).astype(jnp.float32)
    mask_hi = (row == lane_sub + SUBLANES).astype(jnp.float32)

    def tokens(i, _):
        for j in range(PEER_UNROLL):
            _gather_slabs(tbl_vmem, rows_smem, i * PEER_UNROLL + j, glo_ref.at[j], ghi_ref.at[j])
        for j in range(PEER_UNROLL):
            t = i * PEER_UNROLL + j
            x16 = x16_ref[pl.ds(pl.multiple_of(t * PACK_ROWS, PACK_ROWS), PACK_ROWS), :].astype(jnp.bfloat16)
            c = (lax.dot_general(x16, glo_ref[j], _NT, preferred_element_type=jnp.float32) * mask_lo
                 + lax.dot_general(x16, ghi_ref[j], _NT, preferred_element_type=jnp.float32) * mask_hi)
            r_ref[pl.ds(t, 1), :] = jnp.sum(c, axis=0, keepdims=True)
        return 0

    lax.fori_loop(0, tb // PEER_UNROLL, tokens, 0)
    lane_half = (lax.broadcasted_iota(jnp.int32, r_ref.shape, 1) % SUBLANES) // HALF
    own = lane_half.astype(jnp.float32) == _expanded_parity(exp_ref, spread_ref)
    h_ref[...] = lax.dot_general(jnp.where(own, r_ref[...], 0.0), spread_ref[...], _NT,
                                 precision=lax.Precision.HIGHEST, preferred_element_type=jnp.float32)


def _peer_v_kernel(exp_ref, coef_ref, spread_ref, tbl_hbm, o_ref, tbl_vmem, rows_vmem, rows_smem, glo_ref, ghi_ref,
                   cx_ref, px_ref, o8_ref, sem):
    _load_table_and_rows(tbl_hbm, tbl_vmem, exp_ref, rows_vmem, rows_smem, sem)
    cx_ref[...] = jnp.dot(coef_ref[...].astype(jnp.bfloat16), spread_ref[...].astype(jnp.bfloat16),
                          preferred_element_type=jnp.float32)
    px_ref[...] = HALF * _expanded_parity(exp_ref, spread_ref)
    row = lax.broadcasted_iota(jnp.int32, (PACK_ROWS, PEER_ROWS), 0)
    lane_sub = lax.broadcasted_iota(jnp.int32, (PACK_ROWS, PEER_ROWS), 1) % SUBLANES
    offset = jnp.where(row < HALF, lane_sub - row, -1).astype(jnp.float32)

    def tokens(i, _):
        for j in range(PEER_UNROLL):
            _gather_slabs(tbl_vmem, rows_smem, i * PEER_UNROLL + j, glo_ref.at[j], ghi_ref.at[j])
        for j in range(PEER_UNROLL):
            t = i * PEER_UNROLL + j
            ct = jnp.where(offset == px_ref[pl.ds(t, 1), :], cx_ref[pl.ds(t, 1), :], 0.0).astype(jnp.bfloat16)
            lo = jnp.dot(ct, glo_ref[j], preferred_element_type=jnp.float32)
            hi = jnp.dot(ct, ghi_ref[j], preferred_element_type=jnp.float32)
            o8_ref[pl.ds(pl.multiple_of(t * SUBLANES, SUBLANES), SUBLANES), :] = jnp.concatenate(
                [lo[:HALF], hi[:HALF]], axis=0)
        return 0

    tb = coef_ref.shape[0]
    lax.fori_loop(0, tb // PEER_UNROLL, tokens, 0)
    for b in range(SUBLANES):
        o_ref[:, b * LANES:(b + 1) * LANES] = o8_ref[pl.ds(b, tb, stride=SUBLANES), :]


def _peer_call(body, experts, tokens_in, table, out_tail, name):
    t = experts.shape[0]
    tb = PEER_TB
    spread = jnp.repeat(jnp.eye(PEER_SLOTS, dtype=jnp.float32), SUBLANES, axis=1)

    def block(tail):
        return pl.BlockSpec((tb,) + tail, lambda i: (i,) + (0,) * len(tail))

    gathered = pltpu.VMEM((PEER_UNROLL, PEER_ROWS, LANES), jnp.bfloat16)
    expanded = pltpu.VMEM((tb, PEER_ROWS), jnp.float32)
    is_u = body is _peer_u_kernel
    staged = pltpu.VMEM((tb * (PACK_ROWS if is_u else SUBLANES), LANES), jnp.float32)
    return pl.pallas_call(
        body,
        grid=(t // tb,),
        in_specs=[block((PEER_SLOTS,)), block(tokens_in.shape[1:]),
                  pl.BlockSpec(spread.shape, lambda i: (0, 0)), pl.BlockSpec(memory_space=pl.ANY)],
        out_specs=block(out_tail),
        out_shape=jax.ShapeDtypeStruct((t,) + out_tail, jnp.float32),
        scratch_shapes=[pltpu.VMEM(table.shape, table.dtype), pltpu.VMEM((tb, PEER_SLOTS), jnp.int32),
                        pltpu.SMEM((tb, PEER_SLOTS), jnp.int32), gathered, gathered, expanded]
        + ([] if is_u else [expanded]) + [staged, pltpu.SemaphoreType.DMA((2,))],
        compiler_params=pltpu.CompilerParams(dimension_semantics=("arbitrary",), vmem_limit_bytes=VMEM_LIMIT),
        name=name,
    )(experts, tokens_in, spread, table)


def peer_retrieve(x, experts, gates, u_packed, v_packed):
    t, d = x.shape
    h = _peer_call(_peer_u_kernel, experts, x, u_packed, (PEER_SLOTS,), "peer_u")
    coef = gates * jax.nn.gelu(h, approximate=False)
    return _peer_call(_peer_v_kernel, experts, coef, v_packed, (d,), "peer_v")


def rms_norm(x, w):
    xf = x.astype(jnp.float32)
    return xf * lax.rsqrt(jnp.mean(xf * xf, axis=-1, keepdims=True) + RMS_EPS) * w.astype(jnp.float32)


def l2_norm(x):
    return x * lax.rsqrt(jnp.sum(x * x, axis=-1, keepdims=True) + RMS_EPS)


def gather_pages(cache, layer, page_table):
    pages = cache[layer, page_table]
    b, n, p, h, d = pages.shape
    return pages.reshape(b, n * p, h, d)


def gated_delta_rule(q, k, v, g, beta, s0):
    B, L, H, d = q.shape
    c = min(DELTA_CHUNK, L)
    pad = (-L) % c
    if pad:
        pw4 = ((0, 0), (0, pad), (0, 0), (0, 0))
        pw3 = ((0, 0), (0, pad), (0, 0))
        q, k, v = jnp.pad(q, pw4), jnp.pad(k, pw4), jnp.pad(v, pw4)
        g, beta = jnp.pad(g, pw3), jnp.pad(beta, pw3)
    n = (L + pad) // c

    def to_chunks(t):
        return jnp.transpose(t.reshape(B, n, c, H, d), (1, 0, 3, 2, 4))

    qc = to_chunks(q) * ATTN_SCALE
    kc = to_chunks(k)
    vc = to_chunks(v)
    gc = jnp.cumsum(jnp.transpose(g.reshape(B, n, c, H), (1, 0, 3, 2)), axis=-1)
    bc = jnp.transpose(beta.reshape(B, n, c, H), (1, 0, 3, 2))
    incl = jnp.tril(jnp.ones((c, c), bool))
    strict = jnp.tril(jnp.ones((c, c), bool), -1)
    decay = jnp.exp(jnp.where(incl, gc[..., :, None] - gc[..., None, :], NEG))
    kb = kc * bc[..., None]
    lmat = jnp.where(strict, jnp.einsum('nbhid,nbhjd->nbhij', kb, kc) * decay, 0.0)
    eye = jnp.eye(c, dtype=jnp.float32)
    tmat = lax.linalg.triangular_solve(eye + lmat, jnp.broadcast_to(eye, lmat.shape), left_side=True, lower=True)
    u = jnp.einsum('nbhij,nbhjd->nbhid', tmat, vc * bc[..., None])
    w = jnp.einsum('nbhij,nbhjd->nbhid', tmat, kb * jnp.exp(gc)[..., None])
    intra = jnp.where(incl, jnp.einsum('nbhid,nbhjd->nbhij', qc, kc) * decay, 0.0)

    def step(s, xs):
        q_i, k_i, u_i, w_i, g_i, a_i = xs
        v_new = u_i - jnp.einsum('bhcd,bhde->bhce', w_i, s)
        o = (jnp.einsum('bhcd,bhde->bhce', q_i * jnp.exp(g_i)[..., None], s)
             + jnp.einsum('bhij,bhje->bhie', a_i, v_new))
        g_last = g_i[..., -1]
        s = (s * jnp.exp(g_last)[..., None, None]
             + jnp.einsum('bhcd,bhce->bhde', k_i * jnp.exp(g_last[..., None] - g_i)[..., None], v_new))
        return s, o

    s_fin, o = lax.scan(step, s0, (qc, kc, u, w, gc, intra))
    o = jnp.transpose(o, (1, 0, 3, 2, 4)).reshape(B, n * c, H, d)[:, :L]
    return o, s_fin


def deltanet_branch(proj, conv_buf, s0, conv_w, a_log, dt_bias, norm_w):
    B, L, _ = proj.shape
    raw = proj[..., :DN_QKV]
    z = proj[..., DN_QKV:DN_QKV + DN_WIDTH]
    a = proj[..., DN_QKV + DN_WIDTH:DN_QKV + DN_WIDTH + DN_HEADS]
    b = proj[..., DN_QKV + DN_WIDTH + DN_HEADS:DN_IN]
    xp = jnp.concatenate([conv_buf.astype(jnp.float32), raw], axis=1)
    conv = sum(xp[:, i:i + L] * conv_w[i].astype(jnp.float32) for i in range(CONV_W))
    qkv = jax.nn.silu(conv)
    q, k, v = jnp.split(qkv, 3, axis=-1)
    q = l2_norm(q.reshape(B, L, DN_HEADS, HEAD_DIM))
    k = l2_norm(k.reshape(B, L, DN_HEADS, HEAD_DIM))
    v = v.reshape(B, L, DN_HEADS, HEAD_DIM)
    g = -jnp.exp(a_log.astype(jnp.float32)) * jax.nn.softplus(a + dt_bias.astype(jnp.float32))
    beta = jax.nn.sigmoid(b)
    o, s_fin = gated_delta_rule(q, k, v, g, beta, s0.astype(jnp.float32))
    o = rms_norm(o, norm_w) * jax.nn.silu(z.reshape(B, L, DN_HEADS, HEAD_DIM))
    return o.reshape(B, L, DN_WIDTH), xp[:, L:], s_fin


def moba_blocks(k, v):
    B, L, H, d = k.shape
    pad = (-L) % MOBA_BLOCK
    pw = ((0, 0), (0, pad), (0, 0), (0, 0))
    kb = jnp.pad(k.astype(jnp.float32), pw).reshape(B, -1, MOBA_BLOCK, H, d)
    vb = jnp.pad(v.astype(jnp.float32), pw).reshape(B, -1, MOBA_BLOCK, H, d)
    means = jnp.mean(kb, axis=2)
    return kb, vb, means


def moba_attend(q, q_pos, kb, vb, means):
    nb = kb.shape[0]
    n_sel = min(MOBA_TOPK, nb)
    Q, H, _ = q.shape
    own = q_pos // MOBA_BLOCK
    qs = q * ATTN_SCALE
    blk = jnp.einsum('qhd,nhd->qhn', qs, means)
    cand = jnp.arange(nb)[None, None, :] < own[:, None, None]
    top_val, top_idx = lax.top_k(jnp.where(cand, blk, NEG), n_sel)
    sel = jnp.concatenate([top_idx, jnp.broadcast_to(own[:, None, None], (Q, H, 1))], axis=-1)
    slot_ok = jnp.concatenate([top_val > 0.5 * NEG, jnp.ones((Q, H, 1), bool)], axis=-1)
    s_idx = jnp.arange(MOBA_BLOCK)
    h_idx = jnp.arange(H)[None, :, None, None]
    gidx = sel[..., None]
    k_sel = kb[gidx, s_idx, h_idx]
    v_sel = vb[gidx, s_idx, h_idx]
    k_pos = gidx * MOBA_BLOCK + s_idx
    mask = slot_ok[..., None] & (k_pos <= q_pos[:, None, None, None])
    logits = jnp.where(mask, jnp.einsum('qhd,qhjsd->qhjs', qs, k_sel), NEG)
    p = jax.nn.softmax(logits.reshape(Q, H, -1), axis=-1).reshape(logits.shape)
    return jnp.einsum('qhjs,qhjsd->qhd', p, v_sel)


def moba_prompt(q, k, v):
    B, L, H, d = q.shape
    kb, vb, means = moba_blocks(k, v)
    pos = jnp.arange(L, dtype=jnp.int32).reshape(-1, Q_BLOCK)
    qblk = q.reshape(B, -1, Q_BLOCK, H, d)

    def per_seq(xs):
        q_s, kb_s, vb_s, m_s = xs
        return lax.map(lambda qp: moba_attend(qp[0], qp[1], kb_s, vb_s, m_s), (q_s, pos))

    return lax.map(per_seq, (qblk, kb, vb, means)).reshape(B, L, H, d)


def moba_sample(q, k_new, v_new, k_past, v_past):
    past_len = k_past.shape[1]
    kb, vb, means = moba_blocks(jnp.concatenate([k_past.astype(jnp.float32), k_new], axis=1),
                                jnp.concatenate([v_past.astype(jnp.float32), v_new], axis=1))
    q_pos = past_len + jnp.arange(q.shape[1], dtype=jnp.int32)
    return jax.vmap(moba_attend, in_axes=(0, None, 0, 0, 0))(q, q_pos, kb, vb, means)


def sb_segment(q, k, v, q_pos, k_pos, tail):
    z = jnp.einsum('bqhd,bkhd->bhqk', q, k.astype(jnp.float32)) * ATTN_SCALE
    causal = k_pos[None, :] < q_pos[:, None]
    log_1m = jnp.where(causal, jax.nn.log_sigmoid(-z), 0.0)
    later = lax.cumsum(log_1m, axis=3, reverse=True) - log_1m + tail[..., None]
    a = jnp.where(causal, jnp.exp(jax.nn.log_sigmoid(z) + later), 0.0)
    return jnp.einsum('bhqk,bkhd->bqhd', a, v.astype(jnp.float32)), tail + jnp.sum(log_1m, axis=3)


def sb_prompt(q, k, v):
    B, L, H, d = q.shape
    outs = []
    for i in range(L // Q_BLOCK):
        lo, hi = i * Q_BLOCK, (i + 1) * Q_BLOCK
        o, _ = sb_segment(q[:, lo:hi], k[:, :hi], v[:, :hi], jnp.arange(lo, hi), jnp.arange(hi),
                          jnp.zeros((B, H, Q_BLOCK), jnp.float32))
        outs.append(o)
    return jnp.concatenate(outs, axis=1)


def sb_sample(q, k_new, v_new, k_past, v_past):
    B, L, H, d = q.shape
    past_len = k_past.shape[1]
    q_pos = past_len + jnp.arange(L)
    o_new, tail = sb_segment(q, k_new, v_new, q_pos, q_pos, jnp.zeros((B, H, L), jnp.float32))
    o_past, _ = sb_segment(q, k_past, v_past, q_pos, jnp.arange(past_len), tail)
    return o_new + o_past


def peer_ffn(x, w_query, keys1, keys2, u_table, v_table):
    T, D = x.shape
    q = (x @ w_query.astype(jnp.float32)).reshape(T, PEER_HEADS, PEER_QDIM)
    half = PEER_QDIM // 2
    s1, i1 = lax.top_k(jnp.einsum('thd,hnd->thn', q[..., :half], keys1.astype(jnp.float32)), PEER_TOPK)
    s2, i2 = lax.top_k(jnp.einsum('thd,hnd->thn', q[..., half:], keys2.astype(jnp.float32)), PEER_TOPK)
    cand_s = (s1[..., :, None] + s2[..., None, :]).reshape(T, PEER_HEADS, PEER_TOPK * PEER_TOPK)
    cand_i = (i1[..., :, None] * PEER_KEYS + i2[..., None, :]).reshape(T, PEER_HEADS, PEER_TOPK * PEER_TOPK)
    top_s, pos = lax.top_k(cand_s, PEER_TOPK)
    experts = jnp.take_along_axis(cand_i, pos, axis=-1).reshape(T, -1)
    gates = jax.nn.softmax(top_s, axis=-1).reshape(T, -1)
    return peer_retrieve(x, experts.astype(jnp.int32), gates, u_table, v_table)


def even_mixer(h, norm_w, conv_buf, s0, past_kv, w_in, conv_w, a_log, dt_bias, norm_dn, q_norm, k_norm, w_out):
    B, L, _ = h.shape
    tm = 256 if B * L >= 256 else B * L
    proj = norm_proj(h.reshape(B * L, D_MODEL), norm_w, w_in, tm).reshape(B, L, IN_EVEN)
    o_dn, new_buf, s_fin = deltanet_branch(proj[..., :DN_IN], conv_buf, s0, conv_w, a_log, dt_bias, norm_dn)
    mq, mk, mv = jnp.split(proj[..., DN_IN:], 3, axis=-1)
    q = rms_norm(mq.reshape(B, L, MOBA_HEADS, HEAD_DIM), q_norm)
    k = rms_norm(mk.reshape(B, L, MOBA_HEADS, HEAD_DIM), k_norm)
    v = mv.reshape(B, L, MOBA_HEADS, HEAD_DIM)
    if past_kv is None:
        o_moba = moba_prompt_attention(q, k, v)
    else:
        o_moba = moba_sample(q, k, v, past_kv[0], past_kv[1])
    out = jnp.concatenate([o_dn, o_moba.reshape(B, L, MOBA_WIDTH)], axis=-1) @ w_out.astype(jnp.float32)
    return out, k, v, new_buf, s_fin


def odd_mixer(h, norm_w, past_kv, w_in, q_norm, k_norm, w_out):
    B, L, _ = h.shape
    tm = 256 if B * L >= 256 else B * L
    proj = norm_proj(h.reshape(B * L, D_MODEL), norm_w, w_in, tm).reshape(B, L, IN_ODD)
    sq, sk, sv = jnp.split(proj, 3, axis=-1)
    q = rms_norm(sq.reshape(B, L, SB_HEADS, HEAD_DIM), q_norm)
    k = rms_norm(sk.reshape(B, L, SB_HEADS, HEAD_DIM), k_norm)
    v = sv.reshape(B, L, SB_HEADS, HEAD_DIM)
    if past_kv is None:
        o = sb_prompt(q, k, v)
    else:
        o = sb_sample(q, k, v, past_kv[0], past_kv[1])
    return o.reshape(B, L, SB_WIDTH) @ w_out.astype(jnp.float32), k, v


def kernel(x_prompt, x_sample, cache_k_moba, cache_v_moba, state_conv_dn, state_delta_dn, cache_k_sb, cache_v_sb,
           page_table, w_in_even, conv_w_dn, a_log_dn, dt_bias_dn, norm_dn, q_norm_moba, k_norm_moba, w_out_even,
           w_in_odd, q_norm_sb, k_norm_sb, w_out_odd, norm_mix, norm_ffn, peer_w_query, peer_keys1, peer_keys2,
           peer_u, peer_v):
    depth = norm_mix.shape[0]
    packed = [(pack_expert_table(peer_u[l]), pack_expert_table(peer_v[l])) for l in range(depth)]

    def run(x, sample):
        B, L, _ = x.shape
        h = x.astype(jnp.float32)
        k_m, v_m, c_dn, s_dn, k_s, v_s = [], [], [], [], [], []
        for layer in range(depth):
            li = layer // 2
            if layer % 2 == 0:
                if sample:
                    conv_buf, s0 = state_conv_dn[li], state_delta_dn[li]
                    past = (gather_pages(cache_k_moba, li, page_table), gather_pages(cache_v_moba, li, page_table))
                else:
                    conv_buf = jnp.zeros((B, CONV_W - 1, DN_QKV), jnp.float32)
                    s0 = jnp.zeros((B, DN_HEADS, HEAD_DIM, HEAD_DIM), jnp.float32)
                    past = None
                mix, k, v, buf, s = even_mixer(h, norm_mix[layer], conv_buf, s0, past, w_in_even[li], conv_w_dn[li],
                                               a_log_dn[li], dt_bias_dn[li], norm_dn[li], q_norm_moba[li],
                                               k_norm_moba[li], w_out_even[li])
                k_m.append(k)
                v_m.append(v)
                c_dn.append(buf)
                s_dn.append(s)
            else:
                past = ((gather_pages(cache_k_sb, li, page_table), gather_pages(cache_v_sb, li, page_table))
                        if sample else None)
                mix, k, v = odd_mixer(h, norm_mix[layer], past, w_in_odd[li], q_norm_sb[li], k_norm_sb[li],
                                      w_out_odd[li])
                k_s.append(k)
                v_s.append(v)
            h = h + mix
            hf = rms_norm(h, norm_ffn[layer]).reshape(B * L, D_MODEL)
            h = h + peer_ffn(hf, peer_w_query[layer], peer_keys1[layer], peer_keys2[layer], *packed[layer]
                             ).reshape(B, L, D_MODEL)
        return (h, jnp.stack(k_m), jnp.stack(v_m), jnp.stack(c_dn), jnp.stack(s_dn), jnp.stack(k_s), jnp.stack(v_s))

    y_p, kmp, vmp, cdp, sdp, ksp, vsp = run(x_prompt, False)
    y_s, kms, vms, cds, sds, kss, vss = run(x_sample, True)
    return (y_p, y_s, kmp, vmp, cdp, sdp, ksp, vsp, kms, vms, cds, sds, kss, vss)
```

```python
import functools

import jax
import jax.numpy as jnp
from jax import lax
from jax.experimental import pallas as pl
from jax.experimental.pallas import tpu as pltpu

D_MODEL = 1024
HEAD_DIM = 64
DN_HEADS = 8
DN_WIDTH = DN_HEADS * HEAD_DIM
MOBA_HEADS = 8
MOBA_WIDTH = MOBA_HEADS * HEAD_DIM
SB_HEADS = 16
SB_WIDTH = SB_HEADS * HEAD_DIM
CONV_W = 4
DELTA_CHUNK = 64
MOBA_BLOCK = 256
MOBA_TOPK = 3
Q_BLOCK = 128
PEER_KEYS = 128
PEER_HEADS = 8
PEER_TOPK = 16
PEER_QDIM = 128
PEER_CHUNK = 128
RMS_EPS = 1e-6
NEG = -1e30
ATTN_SCALE = HEAD_DIM ** -0.5
DN_QKV = 3 * DN_WIDTH
DN_IN = DN_QKV + DN_WIDTH + 2 * DN_HEADS
IN_EVEN = DN_IN + 3 * MOBA_WIDTH
IN_ODD = 3 * SB_WIDTH

LANES = 128
VMEM_LIMIT = 56 * 1024 * 1024


def _norm_proj_kernel(x_ref, g_ref, w_ref, o_ref):
    x = x_ref[...]
    inv = lax.rsqrt(jnp.mean(x * x, axis=-1, keepdims=True) + RMS_EPS)
    xn = (x * inv * g_ref[...]).astype(jnp.bfloat16)
    o_ref[...] = jnp.dot(xn, w_ref[...], preferred_element_type=jnp.float32)


def norm_proj(x, gain, w, tm):
    m, d = x.shape
    n = w.shape[1]
    n_pad = -(-n // LANES) * LANES
    wb = jnp.pad(w.astype(jnp.bfloat16), ((0, 0), (0, n_pad - n)))
    out = pl.pallas_call(
        _norm_proj_kernel,
        grid=(m // tm,),
        in_specs=[
            pl.BlockSpec((tm, d), lambda i: (i, 0)),
            pl.BlockSpec((1, d), lambda i: (0, 0)),
            pl.BlockSpec((d, n_pad), lambda i: (0, 0)),
        ],
        out_specs=pl.BlockSpec((tm, n_pad), lambda i: (i, 0)),
        out_shape=jax.ShapeDtypeStruct((m, n_pad), jnp.float32),
        compiler_params=pltpu.CompilerParams(
            dimension_semantics=("parallel",), vmem_limit_bytes=VMEM_LIMIT),
        name="norm_proj",
    )(x, gain.reshape(1, d).astype(jnp.float32), wb)
    return out[:, :n]


_NT = (((1,), (1,)), ((), ()))


def _moba_prompt_kernel(q_ref, k_ref, v_ref, o_ref, means_ref, kb_ref, vb_ref):
    i = pl.program_id(2)
    nb = means_ref.shape[0]
    blk_len = MOBA_BLOCK

    @pl.when(i == 0)
    def _():
        for n in range(nb):
            means_ref[n:n + 1, :] = jnp.mean(k_ref[0, 0, n * blk_len:(n + 1) * blk_len, :], axis=0, keepdims=True)
        kb_ref[...] = k_ref[0, 0].astype(jnp.bfloat16)
        vb_ref[...] = v_ref[0, 0].astype(jnp.bfloat16)

    qs = q_ref[0, 0] * ATTN_SCALE
    blk = lax.dot_general(qs, means_ref[...], _NT, precision=lax.Precision.HIGHEST,
                          preferred_element_type=jnp.float32)
    col = lax.broadcasted_iota(jnp.int32, (blk_len, nb), 1)
    cand = col < i
    blk = jnp.where(cand, blk, NEG)
    rank = jnp.zeros((blk_len, nb), jnp.int32)
    for m in range(nb):
        bm = blk[:, m:m + 1]
        rank += ((bm > blk) | ((bm == blk) & (m < col))).astype(jnp.int32)
    sel = (cand & (rank < MOBA_TOPK) & (blk > 0.5 * NEG)).astype(jnp.float32)

    qb = qs.astype(jnp.bfloat16)
    own = pl.multiple_of(i * blk_len, blk_len)
    s = lax.dot_general(qb, kb_ref[pl.ds(own, blk_len), :], _NT, preferred_element_type=jnp.float32)
    row = lax.broadcasted_iota(jnp.int32, (blk_len, blk_len), 0)
    colk = lax.broadcasted_iota(jnp.int32, (blk_len, blk_len), 1)
    s = jnp.where(colk <= row, s, NEG)
    m0 = jnp.max(s, axis=1, keepdims=True)
    p = jnp.exp(s - m0)
    l0 = jnp.sum(p, axis=1, keepdims=True)
    acc0 = jnp.dot(p.astype(jnp.bfloat16), vb_ref[pl.ds(own, blk_len), :], preferred_element_type=jnp.float32)

    def body(n, carry):
        m_i, l_i, acc = carry
        start = pl.multiple_of(n * blk_len, blk_len)
        s = lax.dot_general(qb, kb_ref[pl.ds(start, blk_len), :], _NT, preferred_element_type=jnp.float32)
        chosen = jnp.sum(jnp.where(col == n, sel, 0.0), axis=1, keepdims=True) > 0.5
        s = jnp.where(chosen, s, NEG)
        m_new = jnp.maximum(m_i, jnp.max(s, axis=1, keepdims=True))
        alpha = jnp.exp(m_i - m_new)
        p = jnp.exp(s - m_new)
        l_new = alpha * l_i + jnp.sum(p, axis=1, keepdims=True)
        acc = alpha * acc + jnp.dot(p.astype(jnp.bfloat16), vb_ref[pl.ds(start, blk_len), :],
                                    preferred_element_type=jnp.float32)
        return m_new, l_new, acc

    _, l_f, acc_f = lax.fori_loop(0, i, body, (m0, l0, acc0))
    o_ref[0, 0] = acc_f / l_f


def moba_prompt_attention(q, k, v):
    B, L, H, d = q.shape
    nb = L // MOBA_BLOCK
    qt, kt, vt = (jnp.transpose(t, (0, 2, 1, 3)) for t in (q, k, v))
    full = pl.BlockSpec((1, 1, L, d), lambda b, h, i: (b, h, 0, 0))
    tile = pl.BlockSpec((1, 1, MOBA_BLOCK, d), lambda b, h, i: (b, h, i, 0))
    o = pl.pallas_call(
        _moba_prompt_kernel,
        grid=(B, H, nb),
        in_specs=[tile, full, full],
        out_specs=tile,
        out_shape=jax.ShapeDtypeStruct((B, H, L, d), jnp.float32),
        scratch_shapes=[pltpu.VMEM((nb, d), jnp.float32), pltpu.VMEM((L, d), jnp.bfloat16),
                        pltpu.VMEM((L, d), jnp.bfloat16)],
        compiler_params=pltpu.CompilerParams(
            dimension_semantics=("parallel", "parallel", "arbitrary"), vmem_limit_bytes=VMEM_LIMIT),
        name="moba_prompt",
    )(qt, kt, vt)
    return jnp.transpose(o, (0, 2, 1, 3))


SUBLANES = 8
HALF = SUBLANES // 2
PEER_TB = 128
PEER_SLOTS = PEER_HEADS * PEER_TOPK
PEER_ROWS = PEER_SLOTS * SUBLANES
PACK_ROWS = 2 * SUBLANES
PEER_UNROLL = 2
HI_MASK = 0xFFFF0000


def pack_expert_table(tbl):
    e, d = tbl.shape
    bits = lax.bitcast_convert_type(tbl.astype(jnp.bfloat16), jnp.uint16).astype(jnp.uint32)
    bits = bits.reshape(e // 2, 2, 2, HALF, LANES)
    words = bits[:, :, 0] | (bits[:, :, 1] << 16)
    return words.reshape(e // 2 * SUBLANES, LANES)


def _unpack_pair(w):
    lo = lax.bitcast_convert_type(w << 16, jnp.float32)
    hi = lax.bitcast_convert_type(w & jnp.uint32(HI_MASK), jnp.float32)
    return lo, hi


def _load_table_and_rows(tbl_hbm, tbl_vmem, exp_ref, rows_vmem, rows_smem, sem):
    @pl.when(pl.program_id(0) == 0)
    def _():
        cp = pltpu.make_async_copy(tbl_hbm, tbl_vmem, sem.at[0])
        cp.start()
        cp.wait()

    rows_vmem[...] = (exp_ref[...] >> 1) * SUBLANES
    cp = pltpu.make_async_copy(rows_vmem, rows_smem, sem.at[1])
    cp.start()
    cp.wait()


def _gather_slabs(tbl_vmem, rows_smem, t, glo_ref, ghi_ref):
    for m in range(PEER_SLOTS // 2):
        halves = []
        for k in (2 * m, 2 * m + 1):
            row = pl.multiple_of(rows_smem[t, k], SUBLANES)
            halves.append(_unpack_pair(tbl_vmem[pl.ds(row, SUBLANES), :]))
        rows = pl.ds(m * PACK_ROWS, PACK_ROWS)
        glo_ref[rows, :] = jnp.concatenate([halves[0][0], halves[1][0]], axis=0).astype(jnp.bfloat16)
        ghi_ref[rows, :] = jnp.concatenate([halves[0][1], halves[1][1]], axis=0).astype(jnp.bfloat16)


def _expanded_parity(exp_ref, spread_ref):
    par = (exp_ref[...] & 1).astype(jnp.float32)
    return jnp.dot(par, spread_ref[...], preferred_element_type=jnp.float32)


def _peer_u_kernel(exp_ref, x_ref, spread_ref, tbl_hbm, h_ref, tbl_vmem, rows_vmem, rows_smem, glo_ref, ghi_ref,
                   r_ref, x16_ref, sem):
    _load_table_and_rows(tbl_hbm, tbl_vmem, exp_ref, rows_vmem, rows_smem, sem)
    tb = x_ref.shape[0]
    for r in range(PACK_ROWS):
        blk = r % HALF + HALF * (r // SUBLANES)
        x16_ref[pl.ds(r, tb, stride=PACK_ROWS), :] = x_ref[:, blk * LANES:(blk + 1) * LANES]
    row = lax.broadcasted_iota(jnp.int32, (PACK_ROWS, PEER_ROWS), 0)
    lane_sub = lax.broadcasted_iota(jnp.int32, (PACK_ROWS, PEER_ROWS), 1) % SUBLANES
    mask_lo = (row == lane_sub).astype(jnp.float32)
    mask_hi = (row == lane_sub + SUBLANES).astype(jnp.float32)

    def tokens(i, _):
        for j in range(PEER_UNROLL):
            _gather_slabs(tbl_vmem, rows_smem, i * PEER_UNROLL + j, glo_ref.at[j], ghi_ref.at[j])
        for j in range(PEER_UNROLL):
            t = i * PEER_UNROLL + j
            x16 = x16_ref[pl.ds(pl.multiple_of(t * PACK_ROWS, PACK_ROWS), PACK_ROWS), :].astype(jnp.bfloat16)
            c = (lax.dot_general(x16, glo_ref[j], _NT, preferred_element_type=jnp.float32) * mask_lo
                 + lax.dot_general(x16, ghi_ref[j], _NT, preferred_element_type=jnp.float32) * mask_hi)
            r_ref[pl.ds(t, 1), :] = jnp.sum(c, axis=0, keepdims=True)
        return 0

    lax.fori_loop(0, tb // PEER_UNROLL, tokens, 0)
    lane_half = (lax.broadcasted_iota(jnp.int32, r_ref.shape, 1) % SUBLANES) // HALF
    own = lane_half.astype(jnp.float32) == _expanded_parity(exp_ref, spread_ref)
    h_ref[...] = lax.dot_general(jnp.where(own, r_ref[...], 0.0), spread_ref[...], _NT,
                                 precision=lax.Precision.HIGHEST, preferred_element_type=jnp.float32)


def _peer_v_kernel(exp_ref, coef_ref, spread_ref, tbl_hbm, o_ref, tbl_vmem, rows_vmem, rows_smem, glo_ref, ghi_ref,
                   cx_ref, px_ref, o8_ref, sem):
    _load_table_and_rows(tbl_hbm, tbl_vmem, exp_ref, rows_vmem, rows_smem, sem)
    cx_ref[...] = jnp.dot(coef_ref[...].astype(jnp.bfloat16), spread_ref[...].astype(jnp.bfloat16),
                          preferred_element_type=jnp.float32)
    px_ref[...] = HALF * _expanded_parity(exp_ref, spread_ref)
    row = lax.broadcasted_iota(jnp.int32, (PACK_ROWS, PEER_ROWS), 0)
    lane_sub = lax.broadcasted_iota(jnp.int32, (PACK_ROWS, PEER_ROWS), 1) % SUBLANES
    offset = jnp.where(row < HALF, lane_sub - row, -1).astype(jnp.float32)

    def tokens(i, _):
        for j in range(PEER_UNROLL):
            _gather_slabs(tbl_vmem, rows_smem, i * PEER_UNROLL + j, glo_ref.at[j], ghi_ref.at[j])
        for j in range(PEER_UNROLL):
            t = i * PEER_UNROLL + j
            ct = jnp.where(offset == px_ref[pl.ds(t, 1), :], cx_ref[pl.ds(t, 1), :], 0.0).astype(jnp.bfloat16)
            lo = jnp.dot(ct, glo_ref[j], preferred_element_type=jnp.float32)
            hi = jnp.dot(ct, ghi_ref[j], preferred_element_type=jnp.float32)
            o8_ref[pl.ds(pl.multiple_of(t * SUBLANES, SUBLANES), SUBLANES), :] = jnp.concatenate(
                [lo[:HALF], hi[:HALF]], axis=0)
        return 0

    tb = coef_ref.shape[0]
    lax.fori_loop(0, tb // PEER_UNROLL, tokens, 0)
    for b in range(SUBLANES):
        o_ref[:, b * LANES:(b + 1) * LANES] = o8_ref[pl.ds(b, tb, stride=SUBLANES), :]


def _peer_call(body, experts, tokens_in, table, out_tail, name):
    t = experts.shape[0]
    tb = PEER_TB
    spread = jnp.repeat(jnp.eye(PEER_SLOTS, dtype=jnp.float32), SUBLANES, axis=1)

    def block(tail):
        return pl.BlockSpec((tb,) + tail, lambda i: (i,) + (0,) * len(tail))

    gathered = pltpu.VMEM((PEER_UNROLL, PEER_ROWS, LANES), jnp.bfloat16)
    expanded = pltpu.VMEM((tb, PEER_ROWS), jnp.float32)
    is_u = body is _peer_u_kernel
    staged = pltpu.VMEM((tb * (PACK_ROWS if is_u else SUBLANES), LANES), jnp.float32)
    return pl.pallas_call(
        body,
        grid=(t // tb,),
        in_specs=[block((PEER_SLOTS,)), block(tokens_in.shape[1:]),
                  pl.BlockSpec(spread.shape, lambda i: (0, 0)), pl.BlockSpec(memory_space=pl.ANY)],
        out_specs=block(out_tail),
        out_shape=jax.ShapeDtypeStruct((t,) + out_tail, jnp.float32),
        scratch_shapes=[pltpu.VMEM(table.shape, table.dtype), pltpu.VMEM((tb, PEER_SLOTS), jnp.int32),
                        pltpu.SMEM((tb, PEER_SLOTS), jnp.int32), gathered, gathered, expanded]
        + ([] if is_u else [expanded]) + [staged, pltpu.SemaphoreType.DMA((2,))],
        compiler_params=pltpu.CompilerParams(dimension_semantics=("arbitrary",), vmem_limit_bytes=VMEM_LIMIT),
        name=name,
    )(experts, tokens_in, spread, table)


def peer_retrieve(x, experts, gates, u_packed, v_packed):
    t, d = x.shape
    h = _peer_call(_peer_u_kernel, experts, x, u_packed, (PEER_SLOTS,), "peer_u")
    coef = gates * jax.nn.gelu(h, approximate=False)
    return _peer_call(_peer_v_kernel, experts, coef, v_packed, (d,), "peer_v")


def _moba_sample_kernel(pt_ref, q_ref, kn_ref, vn_ref, kc_ref, vc_ref, o_ref, sums_ref, sel_ref, qbd_ref, m_ref,
                        l_ref, acc_ref):
    phase, p = pl.program_id(1), pl.program_id(2)
    n_blocks = sums_ref.shape[0]
    pages_per_block = MOBA_BLOCK // kc_ref.shape[1]
    n_q = q_ref.shape[1]
    rows = qbd_ref.shape[0]
    row_head = lax.broadcasted_iota(jnp.int32, (rows, MOBA_WIDTH), 0) // n_q
    own_head = row_head == lax.broadcasted_iota(jnp.int32, (rows, MOBA_WIDTH), 1) // HEAD_DIM
    blk = p // pages_per_block

    @pl.when((phase == 0) & (p == 0))
    def _():
        sums_ref[...] = jnp.zeros_like(sums_ref)

    @pl.when(phase == 0)
    def _():
        sums_ref[pl.ds(blk, 1), :] += jnp.sum(kc_ref[0], axis=0, keepdims=True)

    @pl.when((phase == 1) & (p == 0))
    def _():
        q_all = jnp.concatenate([q_ref[0] * ATTN_SCALE] * MOBA_HEADS, axis=0)
        q_own = jnp.where(own_head, q_all, 0.0)
        qbd_ref[...] = q_own.astype(jnp.bfloat16)
        means = sums_ref[...] * (1.0 / MOBA_BLOCK)
        score = lax.dot_general(q_own, means, _NT, precision=lax.Precision.HIGHEST,
                                preferred_element_type=jnp.float32)
        col = lax.broadcasted_iota(jnp.int32, score.shape, 1)
        rank = jnp.zeros(score.shape, jnp.int32)
        for m in range(n_blocks):
            sm = score[:, m:m + 1]
            rank += ((sm > score) | ((sm == score) & (m < col))).astype(jnp.int32)
        sel_ref[...] = ((rank < MOBA_TOPK) & (score > 0.5 * NEG)).astype(jnp.float32)
        s = lax.dot_general(q_own.astype(jnp.bfloat16), kn_ref[0].astype(jnp.bfloat16), _NT,
                            preferred_element_type=jnp.float32)
        key = lax.broadcasted_iota(jnp.int32, s.shape, 1)
        qry = lax.broadcasted_iota(jnp.int32, s.shape, 0) % n_q
        s = jnp.where(key <= qry, s, NEG)
        m0 = jnp.max(s, axis=1, keepdims=True)
        e = jnp.exp(s - m0)
        m_ref[...] = m0
        l_ref[...] = jnp.sum(e, axis=1, keepdims=True)
        acc_ref[...] = jnp.dot(e.astype(jnp.bfloat16), vn_ref[0].astype(jnp.bfloat16),
                               preferred_element_type=jnp.float32)

    @pl.when(phase == 1)
    def _():
        s = lax.dot_general(qbd_ref[...], kc_ref[0].astype(jnp.bfloat16), _NT, preferred_element_type=jnp.float32)
        col = lax.broadcasted_iota(jnp.int32, sel_ref.shape, 1)
        chosen = jnp.sum(jnp.where(col == blk, sel_ref[...], 0.0), axis=1, keepdims=True) > 0.5
        s = jnp.where(chosen, s, NEG)
        m_new = jnp.maximum(m_ref[...], jnp.max(s, axis=1, keepdims=True))
        alpha = jnp.exp(m_ref[...] - m_new)
        e = jnp.exp(s - m_new)
        l_ref[...] = alpha * l_ref[...] + jnp.sum(e, axis=1, keepdims=True)
        acc_ref[...] = alpha * acc_ref[...] + jnp.dot(e.astype(jnp.bfloat16), vc_ref[0].astype(jnp.bfloat16),
                                                      preferred_element_type=jnp.float32)
        m_ref[...] = m_new

    @pl.when((phase == 1) & (p == pl.num_programs(2) - 1))
    def _():
        full = jnp.where(own_head, acc_ref[...] / l_ref[...], 0.0)
        out = full[:n_q]
        for h in range(1, MOBA_HEADS):
            out = out + full[h * n_q:(h + 1) * n_q]
        o_ref[0] = out


def moba_sample_attention(q, k_new, v_new, cache_k, cache_v, page_table):
    b, lq, h, d = q.shape
    n_phys, page = cache_k.shape[:2]
    n_pages = page_table.shape[1]
    assert (n_pages * page) % MOBA_BLOCK == 0 and lq <= MOBA_BLOCK and MOBA_BLOCK % page == 0
    n_blocks = n_pages * page // MOBA_BLOCK
    width = h * d
    pad = PACK_ROWS - lq
    flat = lambda t: t.reshape(b, lq, width)
    kn = jnp.pad(flat(k_new), ((0, 0), (0, pad), (0, 0)))
    vn = jnp.pad(flat(v_new), ((0, 0), (0, pad), (0, 0)))
    seq = lambda rows: pl.BlockSpec((1, rows, width), lambda i, ph, p, pt: (i, 0, 0))
    grid_spec = pltpu.PrefetchScalarGridSpec(
        num_scalar_prefetch=1,
        grid=(b, 2, n_pages),
        in_specs=[seq(lq), seq(PACK_ROWS), seq(PACK_ROWS),
                  pl.BlockSpec((1, page, width), lambda i, ph, p, pt: (pt[i * n_pages + p], 0, 0)),
                  pl.BlockSpec((1, page, width), lambda i, ph, p, pt: (pt[i * n_pages + p * ph], 0, 0))],
        out_specs=seq(lq),
        scratch_shapes=[pltpu.VMEM((n_blocks, width), jnp.float32), pltpu.VMEM((h * lq, n_blocks), jnp.float32),
                        pltpu.VMEM((h * lq, width), jnp.bfloat16), pltpu.VMEM((h * lq, 1), jnp.float32),
                        pltpu.VMEM((h * lq, 1), jnp.float32), pltpu.VMEM((h * lq, width), jnp.float32)])
    o = pl.pallas_call(
        _moba_sample_kernel,
        grid_spec=grid_spec,
        out_shape=jax.ShapeDtypeStruct((b, lq, width), jnp.float32),
        compiler_params=pltpu.CompilerParams(
            dimension_semantics=("parallel", "arbitrary", "arbitrary"), vmem_limit_bytes=VMEM_LIMIT),
        name="moba_sample",
    )(page_table.reshape(-1), flat(q), kn, vn, cache_k.reshape(n_phys, page, width),
      cache_v.reshape(n_phys, page, width))
    return o.reshape(b, lq, h, d)


def rms_norm(x, w):
    xf = x.astype(jnp.float32)
    return xf * lax.rsqrt(jnp.mean(xf * xf, axis=-1, keepdims=True) + RMS_EPS) * w.astype(jnp.float32)


def l2_norm(x):
    return x * lax.rsqrt(jnp.sum(x * x, axis=-1, keepdims=True) + RMS_EPS)


def gather_pages(cache, layer, page_table):
    pages = cache[layer, page_table]
    b, n, p, h, d = pages.shape
    return pages.reshape(b, n * p, h, d)


def unit_lower_inverse(lmat):
    c = lmat.shape[-1]
    eye = jnp.eye(c, dtype=lmat.dtype)
    mm = functools.partial(jnp.matmul, precision=lax.Precision.HIGHEST)
    power = -lmat
    inv = eye + power
    n = 2
    while n < c:
        power = mm(power, power)
        inv = inv + mm(inv, power)
        n *= 2
    return inv


def gated_delta_rule(q, k, v, g, beta, s0):
    B, L, H, d = q.shape
    c = min(DELTA_CHUNK, L)
    pad = (-L) % c
    if pad:
        pw4 = ((0, 0), (0, pad), (0, 0), (0, 0))
        pw3 = ((0, 0), (0, pad), (0, 0))
        q, k, v = jnp.pad(q, pw4), jnp.pad(k, pw4), jnp.pad(v, pw4)
        g, beta = jnp.pad(g, pw3), jnp.pad(beta, pw3)
    n = (L + pad) // c

    def to_chunks(t):
        return jnp.transpose(t.reshape(B, n, c, H, d), (1, 0, 3, 2, 4))

    qc = to_chunks(q) * ATTN_SCALE
    kc = to_chunks(k)
    vc = to_chunks(v)
    gc = jnp.cumsum(jnp.transpose(g.reshape(B, n, c, H), (1, 0, 3, 2)), axis=-1)
    bc = jnp.transpose(beta.reshape(B, n, c, H), (1, 0, 3, 2))
    incl = jnp.tril(jnp.ones((c, c), bool))
    strict = jnp.tril(jnp.ones((c, c), bool), -1)
    decay = jnp.exp(jnp.where(incl, gc[..., :, None] - gc[..., None, :], NEG))
    kb = kc * bc[..., None]
    lmat = jnp.where(strict, jnp.einsum('nbhid,nbhjd->nbhij', kb, kc) * decay, 0.0)
    tmat = unit_lower_inverse(lmat)
    u = jnp.einsum('nbhij,nbhjd->nbhid', tmat, vc * bc[..., None])
    w = jnp.einsum('nbhij,nbhjd->nbhid', tmat, kb * jnp.exp(gc)[..., None])
    intra = jnp.where(incl, jnp.einsum('nbhid,nbhjd->nbhij', qc, kc) * decay, 0.0)

    def step(s, xs):
        q_i, k_i, u_i, w_i, g_i, a_i = xs
        v_new = u_i - jnp.einsum('bhcd,bhde->bhce', w_i, s)
        o = (jnp.einsum('bhcd,bhde->bhce', q_i * jnp.exp(g_i)[..., None], s)
             + jnp.einsum('bhij,bhje->bhie', a_i, v_new))
        g_last = g_i[..., -1]
        s = (s * jnp.exp(g_last)[..., None, None]
             + jnp.einsum('bhcd,bhce->bhde', k_i * jnp.exp(g_last[..., None] - g_i)[..., None], v_new))
        return s, o

    s_fin, o = lax.scan(step, s0, (qc, kc, u, w, gc, intra))
    o = jnp.transpose(o, (1, 0, 3, 2, 4)).reshape(B, n * c, H, d)[:, :L]
    return o, s_fin


def deltanet_branch(proj, conv_buf, s0, conv_w, a_log, dt_bias, norm_w):
    B, L, _ = proj.shape
    raw = proj[..., :DN_QKV]
    z = proj[..., DN_QKV:DN_QKV + DN_WIDTH]
    a = proj[..., DN_QKV + DN_WIDTH:DN_QKV + DN_WIDTH + DN_HEADS]
    b = proj[..., DN_QKV + DN_WIDTH + DN_HEADS:DN_IN]
    xp = jnp.concatenate([conv_buf.astype(jnp.float32), raw], axis=1)
    conv = sum(xp[:, i:i + L] * conv_w[i].astype(jnp.float32) for i in range(CONV_W))
    qkv = jax.nn.silu(conv)
    q, k, v = jnp.split(qkv, 3, axis=-1)
    q = l2_norm(q.reshape(B, L, DN_HEADS, HEAD_DIM))
    k = l2_norm(k.reshape(B, L, DN_HEADS, HEAD_DIM))
    v = v.reshape(B, L, DN_HEADS, HEAD_DIM)
    g = -jnp.exp(a_log.astype(jnp.float32)) * jax.nn.softplus(a + dt_bias.astype(jnp.float32))
    beta = jax.nn.sigmoid(b)
    o, s_fin = gated_delta_rule(q, k, v, g, beta, s0.astype(jnp.float32))
    o = rms_norm(o, norm_w) * jax.nn.silu(z.reshape(B, L, DN_HEADS, HEAD_DIM))
    return o.reshape(B, L, DN_WIDTH), xp[:, L:], s_fin


def sb_segment(q, k, v, q_pos, k_pos, tail):
    z = jnp.einsum('bqhd,bkhd->bhqk', q, k.astype(jnp.float32)) * ATTN_SCALE
    causal = k_pos[None, :] < q_pos[:, None]
    log_1m = jnp.where(causal, jax.nn.log_sigmoid(-z), 0.0)
    later = lax.cumsum(log_1m, axis=3, reverse=True) - log_1m + tail[..., None]
    a = jnp.where(causal, jnp.exp(jax.nn.log_sigmoid(z) + later), 0.0)
    return jnp.einsum('bhqk,bkhd->bqhd', a, v.astype(jnp.float32)), tail + jnp.sum(log_1m, axis=3)


def sb_prompt(q, k, v):
    B, L, H, d = q.shape
    outs = []
    for i in range(L // Q_BLOCK):
        lo, hi = i * Q_BLOCK, (i + 1) * Q_BLOCK
        o, _ = sb_segment(q[:, lo:hi], k[:, :hi], v[:, :hi], jnp.arange(lo, hi), jnp.arange(hi),
                          jnp.zeros((B, H, Q_BLOCK), jnp.float32))
        outs.append(o)
    return jnp.concatenate(outs, axis=1)


def sb_sample(q, k_new, v_new, k_past, v_past):
    B, L, H, d = q.shape
    past_len = k_past.shape[1]
    q_pos = past_len + jnp.arange(L)
    o_new, tail = sb_segment(q, k_new, v_new, q_pos, q_pos, jnp.zeros((B, H, L), jnp.float32))
    o_past, _ = sb_segment(q, k_past, v_past, q_pos, jnp.arange(past_len), tail)
    return o_new + o_past


def peer_ffn(x, w_query, keys1, keys2, u_table, v_table):
    T, D = x.shape
    q = (x @ w_query.astype(jnp.float32)).reshape(T, PEER_HEADS, PEER_QDIM)
    half = PEER_QDIM // 2
    s1, i1 = lax.top_k(jnp.einsum('thd,hnd->thn', q[..., :half], keys1.astype(jnp.float32)), PEER_TOPK)
    s2, i2 = lax.top_k(jnp.einsum('thd,hnd->thn', q[..., half:], keys2.astype(jnp.float32)), PEER_TOPK)
    cand_s = (s1[..., :, None] + s2[..., None, :]).reshape(T, PEER_HEADS, PEER_TOPK * PEER_TOPK)
    cand_i = (i1[..., :, None] * PEER_KEYS + i2[..., None, :]).reshape(T, PEER_HEADS, PEER_TOPK * PEER_TOPK)
    top_s, pos = lax.top_k(cand_s, PEER_TOPK)
    experts = jnp.take_along_axis(cand_i, pos, axis=-1).reshape(T, -1)
    gates = jax.nn.softmax(top_s, axis=-1).reshape(T, -1)
    return peer_retrieve(x, experts.astype(jnp.int32), gates, u_table, v_table)


def even_mixer(h, norm_w, conv_buf, s0, past_kv, w_in, conv_w, a_log, dt_bias, norm_dn, q_norm, k_norm, w_out):
    B, L, _ = h.shape
    tm = 256 if B * L >= 256 else B * L
    proj = norm_proj(h.reshape(B * L, D_MODEL), norm_w, w_in, tm).reshape(B, L, IN_EVEN)
    o_dn, new_buf, s_fin = deltanet_branch(proj[..., :DN_IN], conv_buf, s0, conv_w, a_log, dt_bias, norm_dn)
    mq, mk, mv = jnp.split(proj[..., DN_IN:], 3, axis=-1)
    q = rms_norm(mq.reshape(B, L, MOBA_HEADS, HEAD_DIM), q_norm)
    k = rms_norm(mk.reshape(B, L, MOBA_HEADS, HEAD_DIM), k_norm)
    v = mv.reshape(B, L, MOBA_HEADS, HEAD_DIM)
    if past_kv is None:
        o_moba = moba_prompt_attention(q, k, v)
    else:
        o_moba = moba_sample_attention(q, k, v, *past_kv)
    out = jnp.concatenate([o_dn, o_moba.reshape(B, L, MOBA_WIDTH)], axis=-1) @ w_out.astype(jnp.float32)
    return out, k, v, new_buf, s_fin


def odd_mixer(h, norm_w, past_kv, w_in, q_norm, k_norm, w_out):
    B, L, _ = h.shape
    tm = 256 if B * L >= 256 else B * L
    proj = norm_proj(h.reshape(B * L, D_MODEL), norm_w, w_in, tm).reshape(B, L, IN_ODD)
    sq, sk, sv = jnp.split(proj, 3, axis=-1)
    q = rms_norm(sq.reshape(B, L, SB_HEADS, HEAD_DIM), q_norm)
    k = rms_norm(sk.reshape(B, L, SB_HEADS, HEAD_DIM), k_norm)
    v = sv.reshape(B, L, SB_HEADS, HEAD_DIM)
    if past_kv is None:
        o = sb_prompt(q, k, v)
    else:
        o = sb_sample(q, k, v, past_kv[0], past_kv[1])
    return o.reshape(B, L, SB_WIDTH) @ w_out.astype(jnp.float32), k, v


def kernel(x_prompt, x_sample, cache_k_moba, cache_v_moba, state_conv_dn, state_delta_dn, cache_k_sb, cache_v_sb,
           page_table, w_in_even, conv_w_dn, a_log_dn, dt_bias_dn, norm_dn, q_norm_moba, k_norm_moba, w_out_even,
           w_in_odd, q_norm_sb, k_norm_sb, w_out_odd, norm_mix, norm_ffn, peer_w_query, peer_keys1, peer_keys2,
           peer_u, peer_v):
    depth = norm_mix.shape[0]
    packed = [(pack_expert_table(peer_u[l]), pack_expert_table(peer_v[l])) for l in range(depth)]

    def run(x, sample):
        B, L, _ = x.shape
        h = x.astype(jnp.float32)
        k_m, v_m, c_dn, s_dn, k_s, v_s = [], [], [], [], [], []
        for layer in range(depth):
            li = layer // 2
            if layer % 2 == 0:
                if sample:
                    conv_buf, s0 = state_conv_dn[li], state_delta_dn[li]
                    past = (cache_k_moba[li], cache_v_moba[li], page_table)
                else:
                    conv_buf = jnp.zeros((B, CONV_W - 1, DN_QKV), jnp.float32)
                    s0 = jnp.zeros((B, DN_HEADS, HEAD_DIM, HEAD_DIM), jnp.float32)
                    past = None
                mix, k, v, buf, s = even_mixer(h, norm_mix[layer], conv_buf, s0, past, w_in_even[li], conv_w_dn[li],
                                               a_log_dn[li], dt_bias_dn[li], norm_dn[li], q_norm_moba[li],
                                               k_norm_moba[li], w_out_even[li])
                k_m.append(k)
                v_m.append(v)
                c_dn.append(buf)
                s_dn.append(s)
            else:
                past = ((gather_pages(cache_k_sb, li, page_table), gather_pages(cache_v_sb, li, page_table))
                        if sample else None)
                mix, k, v = odd_mixer(h, norm_mix[layer], past, w_in_odd[li], q_norm_sb[li], k_norm_sb[li],
                                      w_out_odd[li])
                k_s.append(k)
                v_s.append(v)
            h = h + mix
            hf = rms_norm(h, norm_ffn[layer]).reshape(B * L, D_MODEL)
            h = h + peer_ffn(hf, peer_w_query[layer], peer_keys1[layer], peer_keys2[layer], *packed[layer]
                             ).reshape(B, L, D_MODEL)
        return (h, jnp.stack(k_m), jnp.stack(v_m), jnp.stack(c_dn), jnp.stack(s_dn), jnp.stack(k_s), jnp.stack(v_s))

    y_p, kmp, vmp, cdp, sdp, ksp, vsp = run(x_prompt, False)
    y_s, kms, vms, cds, sds, kss, vss = run(x_sample, True)
    return (y_p, y_s, kmp, vmp, cdp, sdp, ksp, vsp, kms, vms, cds, sds, kss, vss)
```

```python
import functools

import jax
import jax.numpy as jnp
from jax import lax
from jax.experimental import pallas as pl
from jax.experimental.pallas import tpu as pltpu

D_MODEL = 1024
HEAD_DIM = 64
DN_HEADS = 8
DN_WIDTH = DN_HEADS * HEAD_DIM
MOBA_HEADS = 8
MOBA_WIDTH = MOBA_HEADS * HEAD_DIM
SB_HEADS = 16
SB_WIDTH = SB_HEADS * HEAD_DIM
CONV_W = 4
DELTA_CHUNK = 64
MOBA_BLOCK = 256
MOBA_TOPK = 3
Q_BLOCK = 128
PEER_KEYS = 128
PEER_HEADS = 8
PEER_TOPK = 16
PEER_QDIM = 128
PEER_CHUNK = 128
RMS_EPS = 1e-6
NEG = -1e30
ATTN_SCALE = HEAD_DIM ** -0.5
DN_QKV = 3 * DN_WIDTH
DN_IN = DN_QKV + DN_WIDTH + 2 * DN_HEADS
IN_EVEN = DN_IN + 3 * MOBA_WIDTH
IN_ODD = 3 * SB_WIDTH

LANES = 128
VMEM_LIMIT = 56 * 1024 * 1024


def _norm_proj_kernel(x_ref, g_ref, w_ref, o_ref):
    x = x_ref[...]
    inv = lax.rsqrt(jnp.mean(x * x, axis=-1, keepdims=True) + RMS_EPS)
    xn = (x * inv * g_ref[...]).astype(jnp.bfloat16)
    o_ref[...] = jnp.dot(xn, w_ref[...], preferred_element_type=jnp.float32)


def norm_proj(x, gain, w, tm):
    m, d = x.shape
    n = w.shape[1]
    n_pad = -(-n // LANES) * LANES
    wb = jnp.pad(w.astype(jnp.bfloat16), ((0, 0), (0, n_pad - n)))
    out = pl.pallas_call(
        _norm_proj_kernel,
        grid=(m // tm,),
        in_specs=[
            pl.BlockSpec((tm, d), lambda i: (i, 0)),
            pl.BlockSpec((1, d), lambda i: (0, 0)),
            pl.BlockSpec((d, n_pad), lambda i: (0, 0)),
        ],
        out_specs=pl.BlockSpec((tm, n_pad), lambda i: (i, 0)),
        out_shape=jax.ShapeDtypeStruct((m, n_pad), jnp.float32),
        compiler_params=pltpu.CompilerParams(
            dimension_semantics=("parallel",), vmem_limit_bytes=VMEM_LIMIT),
        name="norm_proj",
    )(x, gain.reshape(1, d).astype(jnp.float32), wb)
    return out[:, :n]


_NT = (((1,), (1,)), ((), ()))


def _moba_prompt_kernel(q_ref, k_ref, v_ref, o_ref, means_ref, kb_ref, vb_ref):
    i = pl.program_id(2)
    nb = means_ref.shape[0]
    blk_len = MOBA_BLOCK

    @pl.when(i == 0)
    def _():
        for n in range(nb):
            means_ref[n:n + 1, :] = jnp.mean(k_ref[0, 0, n * blk_len:(n + 1) * blk_len, :], axis=0, keepdims=True)
        kb_ref[...] = k_ref[0, 0].astype(jnp.bfloat16)
        vb_ref[...] = v_ref[0, 0].astype(jnp.bfloat16)

    qs = q_ref[0, 0] * ATTN_SCALE
    blk = lax.dot_general(qs, means_ref[...], _NT, precision=lax.Precision.HIGHEST,
                          preferred_element_type=jnp.float32)
    col = lax.broadcasted_iota(jnp.int32, (blk_len, nb), 1)
    cand = col < i
    blk = jnp.where(cand, blk, NEG)
    rank = jnp.zeros((blk_len, nb), jnp.int32)
    for m in range(nb):
        bm = blk[:, m:m + 1]
        rank += ((bm > blk) | ((bm == blk) & (m < col))).astype(jnp.int32)
    sel = (cand & (rank < MOBA_TOPK) & (blk > 0.5 * NEG)).astype(jnp.float32)

    qb = qs.astype(jnp.bfloat16)
    own = pl.multiple_of(i * blk_len, blk_len)
    s = lax.dot_general(qb, kb_ref[pl.ds(own, blk_len), :], _NT, preferred_element_type=jnp.float32)
    row = lax.broadcasted_iota(jnp.int32, (blk_len, blk_len), 0)
    colk = lax.broadcasted_iota(jnp.int32, (blk_len, blk_len), 1)
    s = jnp.where(colk <= row, s, NEG)
    m0 = jnp.max(s, axis=1, keepdims=True)
    p = jnp.exp(s - m0)
    l0 = jnp.sum(p, axis=1, keepdims=True)
    acc0 = jnp.dot(p.astype(jnp.bfloat16), vb_ref[pl.ds(own, blk_len), :], preferred_element_type=jnp.float32)

    def body(n, carry):
        m_i, l_i, acc = carry
        start = pl.multiple_of(n * blk_len, blk_len)
        s = lax.dot_general(qb, kb_ref[pl.ds(start, blk_len), :], _NT, preferred_element_type=jnp.float32)
        chosen = jnp.sum(jnp.where(col == n, sel, 0.0), axis=1, keepdims=True) > 0.5
        s = jnp.where(chosen, s, NEG)
        m_new = jnp.maximum(m_i, jnp.max(s, axis=1, keepdims=True))
        alpha = jnp.exp(m_i - m_new)
        p = jnp.exp(s - m_new)
        l_new = alpha * l_i + jnp.sum(p, axis=1, keepdims=True)
        acc = alpha * acc + jnp.dot(p.astype(jnp.bfloat16), vb_ref[pl.ds(start, blk_len), :],
                                    preferred_element_type=jnp.float32)
        return m_new, l_new, acc

    _, l_f, acc_f = lax.fori_loop(0, i, body, (m0, l0, acc0))
    o_ref[0, 0] = acc_f / l_f


def moba_prompt_attention(q, k, v):
    B, L, H, d = q.shape
    nb = L // MOBA_BLOCK
    qt, kt, vt = (jnp.transpose(t, (0, 2, 1, 3)) for t in (q, k, v))
    full = pl.BlockSpec((1, 1, L, d), lambda b, h, i: (b, h, 0, 0))
    tile = pl.BlockSpec((1, 1, MOBA_BLOCK, d), lambda b, h, i: (b, h, i, 0))
    o = pl.pallas_call(
        _moba_prompt_kernel,
        grid=(B, H, nb),
        in_specs=[tile, full, full],
        out_specs=tile,
        out_shape=jax.ShapeDtypeStruct((B, H, L, d), jnp.float32),
        scratch_shapes=[pltpu.VMEM((nb, d), jnp.float32), pltpu.VMEM((L, d), jnp.bfloat16),
                        pltpu.VMEM((L, d), jnp.bfloat16)],
        compiler_params=pltpu.CompilerParams(
            dimension_semantics=("parallel", "parallel", "arbitrary"), vmem_limit_bytes=VMEM_LIMIT),
        name="moba_prompt",
    )(qt, kt, vt)
    return jnp.transpose(o, (0, 2, 1, 3))


SUBLANES = 8
HALF = SUBLANES // 2
PEER_TB = 128
PEER_SLOTS = PEER_HEADS * PEER_TOPK
PEER_ROWS = PEER_SLOTS * SUBLANES
PACK_ROWS = 2 * SUBLANES
PEER_UNROLL = 8
HI_MASK = 0xFFFF0000


def pack_expert_table(tbl):
    e, d = tbl.shape
    bits = lax.bitcast_convert_type(tbl.astype(jnp.bfloat16), jnp.uint16).astype(jnp.uint32)
    bits = bits.reshape(e // 2, 2, 2, HALF, LANES)
    words = bits[:, :, 0] | (bits[:, :, 1] << 16)
    return words.reshape(e // 2 * SUBLANES, LANES)


def _unpack_pair(w):
    lo = lax.bitcast_convert_type(w << 16, jnp.float32)
    hi = lax.bitcast_convert_type(w & jnp.uint32(HI_MASK), jnp.float32)
    return lo, hi


def _load_table_and_rows(tbl_hbm, tbl_vmem, exp_ref, rows_vmem, rows_smem, sem):
    @pl.when(pl.program_id(0) == 0)
    def _():
        cp = pltpu.make_async_copy(tbl_hbm, tbl_vmem, sem.at[0])
        cp.start()
        cp.wait()

    rows_vmem[...] = (exp_ref[...] >> 1) * SUBLANES
    cp = pltpu.make_async_copy(rows_vmem, rows_smem, sem.at[1])
    cp.start()
    cp.wait()


def _gather_slabs(tbl_vmem, rows_smem, t, glo_ref, ghi_ref):
    for m in range(PEER_SLOTS // 2):
        halves = []
        for k in (2 * m, 2 * m + 1):
            row = pl.multiple_of(rows_smem[t, k], SUBLANES)
            halves.append(_unpack_pair(tbl_vmem[pl.ds(row, SUBLANES), :]))
        rows = pl.ds(m * PACK_ROWS, PACK_ROWS)
        glo_ref[rows, :] = jnp.concatenate([halves[0][0], halves[1][0]], axis=0).astype(jnp.bfloat16)
        ghi_ref[rows, :] = jnp.concatenate([halves[0][1], halves[1][1]], axis=0).astype(jnp.bfloat16)


def _expanded_parity(exp_ref, spread_ref):
    par = (exp_ref[...] & 1).astype(jnp.float32)
    return jnp.dot(par, spread_ref[...], preferred_element_type=jnp.float32)


def _peer_u_kernel(exp_ref, x_ref, spread_ref, tbl_hbm, h_ref, tbl_vmem, rows_vmem, rows_smem, glo_ref, ghi_ref,
                   r_ref, x16_ref, sem):
    _load_table_and_rows(tbl_hbm, tbl_vmem, exp_ref, rows_vmem, rows_smem, sem)
    tb = x_ref.shape[0]
    for r in range(PACK_ROWS):
        blk = r % HALF + HALF * (r // SUBLANES)
        x16_ref[pl.ds(r, tb, stride=PACK_ROWS), :] = x_ref[:, blk * LANES:(blk + 1) * LANES]
    row = lax.broadcasted_iota(jnp.int32, (PACK_ROWS, PEER_ROWS), 0)
    lane_sub = lax.broadcasted_iota(jnp.int32, (PACK_ROWS, PEER_ROWS), 1) % SUBLANES
    mask_lo = (row == lane_sub).astype(jnp.float32)
    mask_hi = (row == lane_sub + SUBLANES).astype(jnp.float32)

    def tokens(i, _):
        for j in range(PEER_UNROLL):
            _gather_slabs(tbl_vmem, rows_smem, i * PEER_UNROLL + j, glo_ref.at[j], ghi_ref.at[j])
        for j in range(PEER_UNROLL):
            t = i * PEER_UNROLL + j
            x16 = x16_ref[pl.ds(pl.multiple_of(t * PACK_ROWS, PACK_ROWS), PACK_ROWS), :].astype(jnp.bfloat16)
            c = (lax.dot_general(x16, glo_ref[j], _NT, preferred_element_type=jnp.float32) * mask_lo
                 + lax.dot_general(x16, ghi_ref[j], _NT, preferred_element_type=jnp.float32) * mask_hi)
            r_ref[pl.ds(t, 1), :] = jnp.sum(c, axis=0, keepdims=True)
        return 0

    lax.fori_loop(0, tb // PEER_UNROLL, tokens, 0)
    lane_half = (lax.broadcasted_iota(jnp.int32, r_ref.shape, 1) % SUBLANES) // HALF
    own = lane_half.astype(jnp.float32) == _expanded_parity(exp_ref, spread_ref)
    h_ref[...] = lax.dot_general(jnp.where(own, r_ref[...], 0.0), spread_ref[...], _NT,
                                 precision=lax.Precision.HIGHEST, preferred_element_type=jnp.float32)


def _peer_v_kernel(exp_ref, coef_ref, spread_ref, tbl_hbm, o_ref, tbl_vmem, rows_vmem, rows_smem, glo_ref, ghi_ref,
                   cx_ref, px_ref, o8_ref, sem):
    _load_table_and_rows(tbl_hbm, tbl_vmem, exp_ref, rows_vmem, rows_smem, sem)
    cx_ref[...] = jnp.dot(coef_ref[...].astype(jnp.bfloat16), spread_ref[...].astype(jnp.bfloat16),
                          preferred_element_type=jnp.float32)
    px_ref[...] = HALF * _expanded_parity(exp_ref, spread_ref)
    row = lax.broadcasted_iota(jnp.int32, (PACK_ROWS, PEER_ROWS), 0)
    lane_sub = lax.broadcasted_iota(jnp.int32, (PACK_ROWS, PEER_ROWS), 1) % SUBLANES
    offset = jnp.where(row < HALF, lane_sub - row, -1).astype(jnp.float32)

    def tokens(i, _):
        for j in range(PEER_UNROLL):
            _gather_slabs(tbl_vmem, rows_smem, i * PEER_UNROLL + j, glo_ref.at[j], ghi_ref.at[j])
        for j in range(PEER_UNROLL):
            t = i * PEER_UNROLL + j
            ct = jnp.where(offset == px_ref[pl.ds(t, 1), :], cx_ref[pl.ds(t, 1), :], 0.0).astype(jnp.bfloat16)
            lo = jnp.dot(ct, glo_ref[j], preferred_element_type=jnp.float32)
            hi = jnp.dot(ct, ghi_ref[j], preferred_element_type=jnp.float32)
            o8_ref[pl.ds(pl.multiple_of(t * SUBLANES, SUBLANES), SUBLANES), :] = jnp.concatenate(
                [lo[:HALF], hi[:HALF]], axis=0)
        return 0

    tb = coef_ref.shape[0]
    lax.fori_loop(0, tb // PEER_UNROLL, tokens, 0)
    for b in range(SUBLANES):
        o_ref[:, b * LANES:(b + 1) * LANES] = o8_ref[pl.ds(b, tb, stride=SUBLANES), :]


def _peer_call(body, experts, tokens_in, table, out_tail, name):
    t = experts.shape[0]
    tb = PEER_TB
    spread = jnp.repeat(jnp.eye(PEER_SLOTS, dtype=jnp.float32), SUBLANES, axis=1)

    def block(tail):
        return pl.BlockSpec((tb,) + tail, lambda i: (i,) + (0,) * len(tail))

    gathered = pltpu.VMEM((PEER_UNROLL, PEER_ROWS, LANES), jnp.bfloat16)
    expanded = pltpu.VMEM((tb, PEER_ROWS), jnp.float32)
    is_u = body is _peer_u_kernel
    staged = pltpu.VMEM((tb * (PACK_ROWS if is_u else SUBLANES), LANES), jnp.float32)
    return pl.pallas_call(
        body,
        grid=(t // tb,),
        in_specs=[block((PEER_SLOTS,)), block(tokens_in.shape[1:]),
                  pl.BlockSpec(spread.shape, lambda i: (0, 0)), pl.BlockSpec(memory_space=pl.ANY)],
        out_specs=block(out_tail),
        out_shape=jax.ShapeDtypeStruct((t,) + out_tail, jnp.float32),
        scratch_shapes=[pltpu.VMEM(table.shape, table.dtype), pltpu.VMEM((tb, PEER_SLOTS), jnp.int32),
                        pltpu.SMEM((tb, PEER_SLOTS), jnp.int32), gathered, gathered, expanded]
        + ([] if is_u else [expanded]) + [staged, pltpu.SemaphoreType.DMA((2,))],
        compiler_params=pltpu.CompilerParams(dimension_semantics=("arbitrary",), vmem_limit_bytes=VMEM_LIMIT),
        name=name,
    )(experts, tokens_in, spread, table)


def peer_retrieve(x, experts, gates, u_packed, v_packed):
    t, d = x.shape
    h = _peer_call(_peer_u_kernel, experts, x, u_packed, (PEER_SLOTS,), "peer_u")
    coef = gates * jax.nn.gelu(h, approximate=False)
    return _peer_call(_peer_v_kernel, experts, coef, v_packed, (d,), "peer_v")


def _moba_sample_kernel(pt_ref, q_ref, kn_ref, vn_ref, kc_ref, vc_ref, o_ref, sums_ref, sel_ref, qbd_ref, m_ref,
                        l_ref, acc_ref):
    phase, p = pl.program_id(1), pl.program_id(2)
    n_blocks = sums_ref.shape[0]
    pages_per_block = MOBA_BLOCK // kc_ref.shape[1]
    n_q = q_ref.shape[1]
    rows = qbd_ref.shape[0]
    row_head = lax.broadcasted_iota(jnp.int32, (rows, MOBA_WIDTH), 0) // n_q
    own_head = row_head == lax.broadcasted_iota(jnp.int32, (rows, MOBA_WIDTH), 1) // HEAD_DIM
    blk = p // pages_per_block

    @pl.when((phase == 0) & (p == 0))
    def _():
        sums_ref[...] = jnp.zeros_like(sums_ref)

    @pl.when(phase == 0)
    def _():
        sums_ref[pl.ds(blk, 1), :] += jnp.sum(kc_ref[0], axis=0, keepdims=True)

    @pl.when((phase == 1) & (p == 0))
    def _():
        q_all = jnp.concatenate([q_ref[0] * ATTN_SCALE] * MOBA_HEADS, axis=0)
        q_own = jnp.where(own_head, q_all, 0.0)
        qbd_ref[...] = q_own.astype(jnp.bfloat16)
        means = sums_ref[...] * (1.0 / MOBA_BLOCK)
        score = lax.dot_general(q_own, means, _NT, precision=lax.Precision.HIGHEST,
                                preferred_element_type=jnp.float32)
        col = lax.broadcasted_iota(jnp.int32, score.shape, 1)
        rank = jnp.zeros(score.shape, jnp.int32)
        for m in range(n_blocks):
            sm = score[:, m:m + 1]
            rank += ((sm > score) | ((sm == score) & (m < col))).astype(jnp.int32)
        sel_ref[...] = ((rank < MOBA_TOPK) & (score > 0.5 * NEG)).astype(jnp.float32)
        s = lax.dot_general(q_own.astype(jnp.bfloat16), kn_ref[0].astype(jnp.bfloat16), _NT,
                            preferred_element_type=jnp.float32)
        key = lax.broadcasted_iota(jnp.int32, s.shape, 1)
        qry = lax.broadcasted_iota(jnp.int32, s.shape, 0) % n_q
        s = jnp.where(key <= qry, s, NEG)
        m0 = jnp.max(s, axis=1, keepdims=True)
        e = jnp.exp(s - m0)
        m_ref[...] = m0
        l_ref[...] = jnp.sum(e, axis=1, keepdims=True)
        acc_ref[...] = jnp.dot(e.astype(jnp.bfloat16), vn_ref[0].astype(jnp.bfloat16),
                               preferred_element_type=jnp.float32)

    @pl.when(phase == 1)
    def _():
        s = lax.dot_general(qbd_ref[...], kc_ref[0].astype(jnp.bfloat16), _NT, preferred_element_type=jnp.float32)
        col = lax.broadcasted_iota(jnp.int32, sel_ref.shape, 1)
        chosen = jnp.sum(jnp.where(col == blk, sel_ref[...], 0.0), axis=1, keepdims=True) > 0.5
        s = jnp.where(chosen, s, NEG)
        m_new = jnp.maximum(m_ref[...], jnp.max(s, axis=1, keepdims=True))
        alpha = jnp.exp(m_ref[...] - m_new)
        e = jnp.exp(s - m_new)
        l_ref[...] = alpha * l_ref[...] + jnp.sum(e, axis=1, keepdims=True)
        acc_ref[...] = alpha * acc_ref[...] + jnp.dot(e.astype(jnp.bfloat16), vc_ref[0].astype(jnp.bfloat16),
                                                      preferred_element_type=jnp.float32)
        m_ref[...] = m_new

    @pl.when((phase == 1) & (p == pl.num_programs(2) - 1))
    def _():
        full = jnp.where(own_head, acc_ref[...] / l_ref[...], 0.0)
        out = full[:n_q]
        for h in range(1, MOBA_HEADS):
            out = out + full[h * n_q:(h + 1) * n_q]
        o_ref[0] = out


def moba_sample_attention(q, k_new, v_new, cache_k, cache_v, page_table):
    b, lq, h, d = q.shape
    n_phys, page = cache_k.shape[:2]
    n_pages = page_table.shape[1]
    assert (n_pages * page) % MOBA_BLOCK == 0 and lq <= MOBA_BLOCK and MOBA_BLOCK % page == 0
    n_blocks = n_pages * page // MOBA_BLOCK
    width = h * d
    pad = PACK_ROWS - lq
    flat = lambda t: t.reshape(b, lq, width)
    kn = jnp.pad(flat(k_new), ((0, 0), (0, pad), (0, 0)))
    vn = jnp.pad(flat(v_new), ((0, 0), (0, pad), (0, 0)))
    seq = lambda rows: pl.BlockSpec((1, rows, width), lambda i, ph, p, pt: (i, 0, 0))
    grid_spec = pltpu.PrefetchScalarGridSpec(
        num_scalar_prefetch=1,
        grid=(b, 2, n_pages),
        in_specs=[seq(lq), seq(PACK_ROWS), seq(PACK_ROWS),
                  pl.BlockSpec((1, page, width), lambda i, ph, p, pt: (pt[i * n_pages + p], 0, 0)),
                  pl.BlockSpec((1, page, width), lambda i, ph, p, pt: (pt[i * n_pages + p * ph], 0, 0))],
        out_specs=seq(lq),
        scratch_shapes=[pltpu.VMEM((n_blocks, width), jnp.float32), pltpu.VMEM((h * lq, n_blocks), jnp.float32),
                        pltpu.VMEM((h * lq, width), jnp.bfloat16), pltpu.VMEM((h * lq, 1), jnp.float32),
                        pltpu.VMEM((h * lq, 1), jnp.float32), pltpu.VMEM((h * lq, width), jnp.float32)])
    o = pl.pallas_call(
        _moba_sample_kernel,
        grid_spec=grid_spec,
        out_shape=jax.ShapeDtypeStruct((b, lq, width), jnp.float32),
        compiler_params=pltpu.CompilerParams(
            dimension_semantics=("parallel", "arbitrary", "arbitrary"), vmem_limit_bytes=VMEM_LIMIT),
        name="moba_sample",
    )(page_table.reshape(-1), flat(q), kn, vn, cache_k.reshape(n_phys, page, width),
      cache_v.reshape(n_phys, page, width))
    return o.reshape(b, lq, h, d)


ROUTE_TB = 256


def _column_top_k(x, k, payload=None):
    n = x.shape[0]
    row = lax.broadcasted_iota(jnp.int32, x.shape, 0)
    vals, picks = [], []
    for _ in range(k):
        best = jnp.max(x, axis=0, keepdims=True)
        pos = jnp.min(jnp.where(x == best, row, n), axis=0, keepdims=True)
        hit = row == pos
        vals.append(best)
        picks.append(pos if payload is None else jnp.sum(jnp.where(hit, payload, 0), axis=0, keepdims=True))
        x = jnp.where(hit, -jnp.inf, x)
    return jnp.concatenate(vals, axis=0), jnp.concatenate(picks, axis=0)


def _peer_route_kernel(q_ref, k1_ref, k2_ref, exp_ref, gate_ref):
    half = PEER_QDIM // 2
    experts, gates = [], []
    for h in range(PEER_HEADS):
        q1 = q_ref[:, h * PEER_QDIM:h * PEER_QDIM + half]
        q2 = q_ref[:, h * PEER_QDIM + half:(h + 1) * PEER_QDIM]
        s1 = lax.dot_general(k1_ref[h], q1, _NT, precision=lax.Precision.HIGHEST, preferred_element_type=jnp.float32)
        s2 = lax.dot_general(k2_ref[h], q2, _NT, precision=lax.Precision.HIGHEST, preferred_element_type=jnp.float32)
        v1, i1 = _column_top_k(s1, PEER_TOPK)
        v2, i2 = _column_top_k(s2, PEER_TOPK)
        cand_s = jnp.concatenate([v1[a:a + 1] + v2 for a in range(PEER_TOPK)], axis=0)
        cand_e = jnp.concatenate([i1[a:a + 1] * PEER_KEYS + i2 for a in range(PEER_TOPK)], axis=0)
        top_s, top_e = _column_top_k(cand_s, PEER_TOPK, payload=cand_e)
        e = jnp.exp(top_s - jnp.max(top_s, axis=0, keepdims=True))
        experts.append(top_e)
        gates.append(e / jnp.sum(e, axis=0, keepdims=True))
    exp_ref[...] = jnp.concatenate(experts, axis=0).T
    gate_ref[...] = jnp.concatenate(gates, axis=0).T


def peer_route(q, keys1, keys2):
    t = q.shape[0]
    tb = min(ROUTE_TB, t)
    keys = pl.BlockSpec(keys1.shape, lambda i: (0, 0, 0))
    out = pl.BlockSpec((tb, PEER_SLOTS), lambda i: (i, 0))
    return pl.pallas_call(
        _peer_route_kernel,
        grid=(t // tb,),
        in_specs=[pl.BlockSpec((tb, q.shape[1]), lambda i: (i, 0)), keys, keys],
        out_specs=(out, out),
        out_shape=(jax.ShapeDtypeStruct((t, PEER_SLOTS), jnp.int32), jax.ShapeDtypeStruct((t, PEER_SLOTS), jnp.float32)),
        compiler_params=pltpu.CompilerParams(dimension_semantics=("parallel",), vmem_limit_bytes=VMEM_LIMIT),
        name="peer_route",
    )(q, keys1.astype(jnp.float32), keys2.astype(jnp.float32))


def rms_norm(x, w):
    xf = x.astype(jnp.float32)
    return xf * lax.rsqrt(jnp.mean(xf * xf, axis=-1, keepdims=True) + RMS_EPS) * w.astype(jnp.float32)


def l2_norm(x):
    return x * lax.rsqrt(jnp.sum(x * x, axis=-1, keepdims=True) + RMS_EPS)


def gather_pages(cache, layer, page_table):
    pages = cache[layer, page_table]
    b, n, p, h, d = pages.shape
    return pages.reshape(b, n * p, h, d)


def unit_lower_inverse(lmat):
    c = lmat.shape[-1]
    eye = jnp.eye(c, dtype=lmat.dtype)
    mm = functools.partial(jnp.matmul, precision=lax.Precision.HIGHEST)
    power = -lmat
    inv = eye + power
    n = 2
    while n < c:
        power = mm(power, power)
        inv = inv + mm(inv, power)
        n *= 2
    return inv


def gated_delta_rule(q, k, v, g, beta, s0):
    B, L, H, d = q.shape
    c = min(DELTA_CHUNK, L)
    pad = (-L) % c
    if pad:
        pw4 = ((0, 0), (0, pad), (0, 0), (0, 0))
        pw3 = ((0, 0), (0, pad), (0, 0))
        q, k, v = jnp.pad(q, pw4), jnp.pad(k, pw4), jnp.pad(v, pw4)
        g, beta = jnp.pad(g, pw3), jnp.pad(beta, pw3)
    n = (L + pad) // c

    def to_chunks(t):
        return jnp.transpose(t.reshape(B, n, c, H, d), (1, 0, 3, 2, 4))

    qc = to_chunks(q) * ATTN_SCALE
    kc = to_chunks(k)
    vc = to_chunks(v)
    gc = jnp.cumsum(jnp.transpose(g.reshape(B, n, c, H), (1, 0, 3, 2)), axis=-1)
    bc = jnp.transpose(beta.reshape(B, n, c, H), (1, 0, 3, 2))
    incl = jnp.tril(jnp.ones((c, c), bool))
    strict = jnp.tril(jnp.ones((c, c), bool), -1)
    decay = jnp.exp(jnp.where(incl, gc[..., :, None] - gc[..., None, :], NEG))
    kb = kc * bc[..., None]
    lmat = jnp.where(strict, jnp.einsum('nbhid,nbhjd->nbhij', kb, kc) * decay, 0.0)
    tmat = unit_lower_inverse(lmat)
    u = jnp.einsum('nbhij,nbhjd->nbhid', tmat, vc * bc[..., None])
    w = jnp.einsum('nbhij,nbhjd->nbhid', tmat, kb * jnp.exp(gc)[..., None])
    intra = jnp.where(incl, jnp.einsum('nbhid,nbhjd->nbhij', qc, kc) * decay, 0.0)

    def step(s, xs):
        q_i, k_i, u_i, w_i, g_i, a_i = xs
        v_new = u_i - jnp.einsum('bhcd,bhde->bhce', w_i, s)
        o = (jnp.einsum('bhcd,bhde->bhce', q_i * jnp.exp(g_i)[..., None], s)
             + jnp.einsum('bhij,bhje->bhie', a_i, v_new))
        g_last = g_i[..., -1]
        s = (s * jnp.exp(g_last)[..., None, None]
             + jnp.einsum('bhcd,bhce->bhde', k_i * jnp.exp(g_last[..., None] - g_i)[..., None], v_new))
        return s, o

    s_fin, o = lax.scan(step, s0, (qc, kc, u, w, gc, intra))
    o = jnp.transpose(o, (1, 0, 3, 2, 4)).reshape(B, n * c, H, d)[:, :L]
    return o, s_fin


def deltanet_branch(proj, conv_buf, s0, conv_w, a_log, dt_bias, norm_w):
    B, L, _ = proj.shape
    raw = proj[..., :DN_QKV]
    z = proj[..., DN_QKV:DN_QKV + DN_WIDTH]
    a = proj[..., DN_QKV + DN_WIDTH:DN_QKV + DN_WIDTH + DN_HEADS]
    b = proj[..., DN_QKV + DN_WIDTH + DN_HEADS:DN_IN]
    xp = jnp.concatenate([conv_buf.astype(jnp.float32), raw], axis=1)
    conv = sum(xp[:, i:i + L] * conv_w[i].astype(jnp.float32) for i in range(CONV_W))
    qkv = jax.nn.silu(conv)
    q, k, v = jnp.split(qkv, 3, axis=-1)
    q = l2_norm(q.reshape(B, L, DN_HEADS, HEAD_DIM))
    k = l2_norm(k.reshape(B, L, DN_HEADS, HEAD_DIM))
    v = v.reshape(B, L, DN_HEADS, HEAD_DIM)
    g = -jnp.exp(a_log.astype(jnp.float32)) * jax.nn.softplus(a + dt_bias.astype(jnp.float32))
    beta = jax.nn.sigmoid(b)
    o, s_fin = gated_delta_rule(q, k, v, g, beta, s0.astype(jnp.float32))
    o = rms_norm(o, norm_w) * jax.nn.silu(z.reshape(B, L, DN_HEADS, HEAD_DIM))
    return o.reshape(B, L, DN_WIDTH), xp[:, L:], s_fin


def sb_segment(q, k, v, q_pos, k_pos, tail):
    z = jnp.einsum('bqhd,bkhd->bhqk', q, k.astype(jnp.float32)) * ATTN_SCALE
    causal = k_pos[None, :] < q_pos[:, None]
    log_1m = jnp.where(causal, jax.nn.log_sigmoid(-z), 0.0)
    later = lax.cumsum(log_1m, axis=3, reverse=True) - log_1m + tail[..., None]
    a = jnp.where(causal, jnp.exp(jax.nn.log_sigmoid(z) + later), 0.0)
    return jnp.einsum('bhqk,bkhd->bqhd', a, v.astype(jnp.float32)), tail + jnp.sum(log_1m, axis=3)


def sb_prompt(q, k, v):
    B, L, H, d = q.shape
    outs = []
    for i in range(L // Q_BLOCK):
        lo, hi = i * Q_BLOCK, (i + 1) * Q_BLOCK
        o, _ = sb_segment(q[:, lo:hi], k[:, :hi], v[:, :hi], jnp.arange(lo, hi), jnp.arange(hi),
                          jnp.zeros((B, H, Q_BLOCK), jnp.float32))
        outs.append(o)
    return jnp.concatenate(outs, axis=1)


def sb_sample(q, k_new, v_new, k_past, v_past):
    B, L, H, d = q.shape
    past_len = k_past.shape[1]
    q_pos = past_len + jnp.arange(L)
    o_new, tail = sb_segment(q, k_new, v_new, q_pos, q_pos, jnp.zeros((B, H, L), jnp.float32))
    o_past, _ = sb_segment(q, k_past, v_past, q_pos, jnp.arange(past_len), tail)
    return o_new + o_past


def peer_ffn(x, w_query, keys1, keys2, u_table, v_table):
    T, D = x.shape
    experts, gates = peer_route(x @ w_query.astype(jnp.float32), keys1, keys2)
    return peer_retrieve(x, experts, gates, u_table, v_table)


def even_mixer(h, norm_w, conv_buf, s0, past_kv, w_in, conv_w, a_log, dt_bias, norm_dn, q_norm, k_norm, w_out):
    B, L, _ = h.shape
    tm = 256 if B * L >= 256 else B * L
    proj = norm_proj(h.reshape(B * L, D_MODEL), norm_w, w_in, tm).reshape(B, L, IN_EVEN)
    o_dn, new_buf, s_fin = deltanet_branch(proj[..., :DN_IN], conv_buf, s0, conv_w, a_log, dt_bias, norm_dn)
    mq, mk, mv = jnp.split(proj[..., DN_IN:], 3, axis=-1)
    q = rms_norm(mq.reshape(B, L, MOBA_HEADS, HEAD_DIM), q_norm)
    k = rms_norm(mk.reshape(B, L, MOBA_HEADS, HEAD_DIM), k_norm)
    v = mv.reshape(B, L, MOBA_HEADS, HEAD_DIM)
    if past_kv is None:
        o_moba = moba_prompt_attention(q, k, v)
    else:
        o_moba = moba_sample_attention(q, k, v, *past_kv)
    out = jnp.concatenate([o_dn, o_moba.reshape(B, L, MOBA_WIDTH)], axis=-1) @ w_out.astype(jnp.float32)
    return out, k, v, new_buf, s_fin


def odd_mixer(h, norm_w, past_kv, w_in, q_norm, k_norm, w_out):
    B, L, _ = h.shape
    tm = 256 if B * L >= 256 else B * L
    proj = norm_proj(h.reshape(B * L, D_MODEL), norm_w, w_in, tm).reshape(B, L, IN_ODD)
    sq, sk, sv = jnp.split(proj, 3, axis=-1)
    q = rms_norm(sq.reshape(B, L, SB_HEADS, HEAD_DIM), q_norm)
    k = rms_norm(sk.reshape(B, L, SB_HEADS, HEAD_DIM), k_norm)
    v = sv.reshape(B, L, SB_HEADS, HEAD_DIM)
    if past_kv is None:
        o = sb_prompt(q, k, v)
    else:
        o = sb_sample(q, k, v, past_kv[0], past_kv[1])
    return o.reshape(B, L, SB_WIDTH) @ w_out.astype(jnp.float32), k, v


def kernel(x_prompt, x_sample, cache_k_moba, cache_v_moba, state_conv_dn, state_delta_dn, cache_k_sb, cache_v_sb,
           page_table, w_in_even, conv_w_dn, a_log_dn, dt_bias_dn, norm_dn, q_norm_moba, k_norm_moba, w_out_even,
           w_in_odd, q_norm_sb, k_norm_sb, w_out_odd, norm_mix, norm_ffn, peer_w_query, peer_keys1, peer_keys2,
           peer_u, peer_v):
    depth = norm_mix.shape[0]
    packed = [(pack_expert_table(peer_u[l]), pack_expert_table(peer_v[l])) for l in range(depth)]

    def run(x, sample):
        B, L, _ = x.shape
        h = x.astype(jnp.float32)
        k_m, v_m, c_dn, s_dn, k_s, v_s = [], [], [], [], [], []
        for layer in range(depth):
            li = layer // 2
            if layer % 2 == 0:
                if sample:
                    conv_buf, s0 = state_conv_dn[li], state_delta_dn[li]
                    past = (cache_k_moba[li], cache_v_moba[li], page_table)
                else:
                    conv_buf = jnp.zeros((B, CONV_W - 1, DN_QKV), jnp.float32)
                    s0 = jnp.zeros((B, DN_HEADS, HEAD_DIM, HEAD_DIM), jnp.float32)
                    past = None
                mix, k, v, buf, s = even_mixer(h, norm_mix[layer], conv_buf, s0, past, w_in_even[li], conv_w_dn[li],
                                               a_log_dn[li], dt_bias_dn[li], norm_dn[li], q_norm_moba[li],
                                               k_norm_moba[li], w_out_even[li])
                k_m.append(k)
                v_m.append(v)
                c_dn.append(buf)
                s_dn.append(s)
            else:
                past = ((gather_pages(cache_k_sb, li, page_table), gather_pages(cache_v_sb, li, page_table))
                        if sample else None)
                mix, k, v = odd_mixer(h, norm_mix[layer], past, w_in_odd[li], q_norm_sb[li], k_norm_sb[li],
                                      w_out_odd[li])
                k_s.append(k)
                v_s.append(v)
            h = h + mix
            hf = rms_norm(h, norm_ffn[layer]).reshape(B * L, D_MODEL)
            h = h + peer_ffn(hf, peer_w_query[layer], peer_keys1[layer], peer_keys2[layer], *packed[layer]
                             ).reshape(B, L, D_MODEL)
        return (h, jnp.stack(k_m), jnp.stack(v_m), jnp.stack(c_dn), jnp.stack(s_dn), jnp.stack(k_s), jnp.stack(v_s))

    y_p, kmp, vmp, cdp, sdp, ksp, vsp = run(x_prompt, False)
    y_s, kms, vms, cds, sds, kss, vss = run(x_sample, True)
    return (y_p, y_s, kmp, vmp, cdp, sdp, ksp, vsp, kms, vms, cds, sds, kss, vss)
```
